```python
import math
import jax, jax.numpy as jnp
from jax import lax
import numpy as np

D_MODEL = 2048
BATCH = 8
SEQ = 2048
DEPTH = 2
DEC_BATCH = 8
DEC_SEQ = 4096
PAST_LEN = 128

CHUNK = 64
CONV_K = 5
GDN_HEADS = 6
GDN_DK = 128
GDN_DV = 128
RET_HEADS = 4
RET_DK = 128
RET_DV = 128
SSD_HEADS = 12
SSD_P = 64
SSD_N = 128
SSD_G = 2
D_FF = 5632
ROPE_BASE = 10000.0
LN_EPS = 1e-5
RMS_EPS = 1e-6

GDN_W = GDN_HEADS * GDN_DV
RET_W = RET_HEADS * RET_DV
SSD_W = SSD_HEADS * SSD_P
MIX_W = GDN_W + RET_W + SSD_W
GDN_QKV_W = 2 * GDN_HEADS * GDN_DK + GDN_W
RET_PROJ_W = 2 * RET_HEADS * RET_DK + 2 * RET_W
SSD_CONV_W = SSD_W + 2 * SSD_G * SSD_N
IN_SIZES = (GDN_QKV_W, GDN_W, 2 * GDN_HEADS, 2 * GDN_HEADS, RET_PROJ_W, SSD_CONV_W, SSD_W, 2 * SSD_HEADS)
IN_W = GDN_QKV_W + GDN_W + 4 * GDN_HEADS + RET_PROJ_W + SSD_CONV_W + SSD_W + 2 * SSD_HEADS
DEEPNORM_ALPHA = (2 * DEPTH) ** 0.25
DEEPNORM_BETA = (8 * DEPTH) ** -0.25

kernel_name = 'hybrid_bidir_gdn_retention_ssd_macaron'


def layer_norm(x, g, b):
    xf = x.astype(jnp.float32)
    mu = jnp.mean(xf, -1, keepdims=True)
    var = jnp.mean(jnp.square(xf - mu), -1, keepdims=True)
    return ((xf - mu) * lax.rsqrt(var + LN_EPS) * g.astype(jnp.float32) + b.astype(jnp.float32)).astype(x.dtype)


def rms_norm(x, w):
    return x * lax.rsqrt(jnp.mean(jnp.square(x), -1, keepdims=True) + RMS_EPS) * w.astype(jnp.float32)


def l2norm(t):
    return t * lax.rsqrt(jnp.sum(jnp.square(t), -1, keepdims=True) + RMS_EPS)


def flip(t):
    return jnp.flip(t, axis=1)


def swiglu(h, w_gate, w_up, w_down):
    return (jax.nn.silu(h @ w_gate) * (h @ w_up)) @ w_down


def depthwise_conv_centred(x, w):
    pad = CONV_K // 2
    return lax.conv_general_dilated(x, w[:, None, :], window_strides=(1,), padding=[(pad, pad)],
                                    dimension_numbers=('NWC', 'WIO', 'NWC'), feature_group_count=x.shape[-1])


def rotary(t):
    s, d = t.shape[1], t.shape[-1]
    inv = ROPE_BASE ** (-jnp.arange(0, d, 2, dtype=jnp.float32) / d)
    ang = jnp.arange(s, dtype=jnp.float32)[:, None] * inv[None, :]
    cos = jnp.cos(ang)[None, :, None, :]
    sin = jnp.sin(ang)[None, :, None, :]
    t1, t2 = t[..., : d // 2], t[..., d // 2:]
    return jnp.concatenate([t1 * cos - t2 * sin, t1 * sin + t2 * cos], -1)


def heads_chunks(t, n):
    t = jnp.moveaxis(t, 2, 1)
    return t.reshape(t.shape[0], t.shape[1], n, CHUNK, *t.shape[3:])


def gated_delta_chunked(q, k, v, g, beta):
    bsz, s, h, dk = q.shape
    dv = v.shape[-1]
    n = s // CHUNK
    q, k, v, g, beta = (heads_chunks(t, n) for t in (q, k, v, g, beta))
    g_cum = jnp.cumsum(g, axis=-1)
    incl = jnp.tril(jnp.ones((CHUNK, CHUNK), dtype=bool))
    strict = jnp.tril(jnp.ones((CHUNK, CHUNK), dtype=bool), -1)
    diff = g_cum[..., :, None] - g_cum[..., None, :]
    decay = jnp.where(incl, jnp.exp(jnp.where(incl, diff, 0.0)), 0.0)
    k_beta = k * beta[..., None]
    a_mat = jnp.eye(CHUNK, dtype=q.dtype) + jnp.where(strict, jnp.einsum('bhnid,bhnjd->bhnij', k_beta, k) * decay, 0.0)
    rhs = jnp.concatenate([v * beta[..., None], k_beta * jnp.exp(g_cum)[..., None]], -1)
    sol = lax.linalg.triangular_solve(a_mat, rhs, left_side=True, lower=True)
    u, w = sol[..., :dv], sol[..., dv:]
    qk = jnp.einsum('bhnid,bhnjd->bhnij', q, k) * decay
    q_dec = q * jnp.exp(g_cum)[..., None]
    k_dec = k * jnp.exp(g_cum[..., -1:] - g_cum)[..., None]
    g_last = jnp.exp(g_cum[..., -1])

    def step(state, inp):
        qk_c, u_c, w_c, q_c, k_c, gl_c = inp
        v_new = u_c - jnp.einsum('bhid,bhde->bhie', w_c, state)
        out = jnp.einsum('bhid,bhde->bhie', q_c, state) + jnp.einsum('bhij,bhje->bhie', qk_c, v_new)
        state = state * gl_c[..., None, None] + jnp.einsum('bhid,bhie->bhde', k_c, v_new)
        return state, out

    xs = tuple(jnp.moveaxis(t, 2, 0) for t in (qk, u, w, q_dec, k_dec, g_last))
    _, out = lax.scan(step, jnp.zeros((bsz, h, dk, dv), q.dtype), xs)
    out = jnp.moveaxis(out, 0, 2).reshape(bsz, h, s, dv)
    return jnp.moveaxis(out, 1, 2)


def retention_chunked(q, k, v, log_gamma, include_diag):
    bsz, s, h, dk = q.shape
    dv = v.shape[-1]
    n = s // CHUNK
    q, k, v = (heads_chunks(t, n) for t in (q, k, v))
    pos = jnp.arange(CHUNK, dtype=jnp.float32)
    diff = pos[:, None] - pos[None, :]
    mask = (diff >= 0) if include_diag else (diff > 0)
    lg = log_gamma[:, None, None]
    dmat = jnp.where(mask, jnp.exp(jnp.where(mask, diff, 0.0) * lg), 0.0)
    qk = jnp.einsum('bhnid,bhnjd->bhnij', q, k) * dmat[:, None]
    intra = jnp.einsum('bhnij,bhnje->bhnie', qk, v)
    q_dec = q * jnp.exp((pos + 1.0)[None, :] * log_gamma[:, None])[:, None, :, None]
    k_dec = k * jnp.exp((CHUNK - 1.0 - pos)[None, :] * log_gamma[:, None])[:, None, :, None]
    kv = jnp.einsum('bhnid,bhnie->bhnde', k_dec, v)
    chunk_decay = jnp.exp(CHUNK * log_gamma)[:, None, None]

    def step(state, kv_c):
        return state * chunk_decay + kv_c, state

    _, prev = lax.scan(step, jnp.zeros((bsz, h, dk, dv), q.dtype), jnp.moveaxis(kv, 2, 0))
    prev = jnp.moveaxis(prev, 0, 2)
    inter = jnp.einsum('bhnid,bhnde->bhnie', q_dec, prev)
    out = (intra + inter).reshape(bsz, h, s, dv)
    return jnp.moveaxis(out, 1, 2)


def ssd_chunked(x, dt, a, bm, cm):
    bsz, s, h, p = x.shape
    g, nst = bm.shape[2], bm.shape[3]
    hg = h // g
    n = s // CHUNK
    x = x.reshape(bsz, n, CHUNK, g, hg, p)
    dt = dt.reshape(bsz, n, CHUNK, g, hg)
    bm = bm.reshape(bsz, n, CHUNK, g, nst)
    cm = cm.reshape(bsz, n, CHUNK, g, nst)
    la_cum = jnp.cumsum(dt * a.reshape(g, hg), axis=2)
    xdt = x * dt[..., None]
    lc = jnp.moveaxis(la_cum, 2, -1)
    incl = jnp.tril(jnp.ones((CHUNK, CHUNK), dtype=bool))
    diff = lc[..., :, None] - lc[..., None, :]
    seg = jnp.where(incl, jnp.exp(jnp.where(incl, diff, 0.0)), 0.0)
    cb = jnp.einsum('bcigs,bcjgs->bcgij', cm, bm)
    y_diag = jnp.einsum('bcgij,bcghij,bcjghp->bcighp', cb, seg, xdt)
    decay_end = jnp.exp(la_cum[:, :, -1:] - la_cum)
    states = jnp.einsum('bcjgs,bcjgh,bcjghp->bcghps', bm, decay_end, xdt)
    chunk_decay = jnp.exp(la_cum[:, :, -1])

    def step(state, inp):
        st_c, dec_c = inp
        return state * dec_c[..., None, None] + st_c, state

    _, prev = lax.scan(step, jnp.zeros((bsz, g, hg, p, nst), x.dtype),
                       (jnp.moveaxis(states, 1, 0), jnp.moveaxis(chunk_decay, 1, 0)))
    prev = jnp.moveaxis(prev, 0, 1)
    y_off = jnp.einsum('bcigs,bcghps,bcigh->bcighp', cm, prev, jnp.exp(la_cum))
    return (y_diag + y_off).reshape(bsz, s, h, p)


def hybrid_mixer(h, w_in, w_out, gdn_conv_w, gdn_a_log, gdn_dt_bias, gdn_norm_w, ret_norm_g, ret_norm_b,
                 ssd_conv_w, ssd_conv_b, ssd_a_log, ssd_dt_bias, ssd_d, ssd_norm_w):
    f32 = jnp.float32
    bsz, s, _ = h.shape
    proj = (h @ w_in).astype(f32)
    pieces = []
    off = 0
    for size in IN_SIZES:
        pieces.append(proj[..., off:off + size])
        off += size
    gdn_qkv, gdn_z, gdn_b, gdn_a, ret_qkvg, ssd_xbc, ssd_z, ssd_dt = pieces

    nq = GDN_HEADS * GDN_DK
    qkv = jax.nn.silu(depthwise_conv_centred(gdn_qkv, gdn_conv_w.astype(f32)))
    gq = l2norm(qkv[..., :nq].reshape(bsz, s, GDN_HEADS, GDN_DK)) * GDN_DK ** -0.5
    gk = l2norm(qkv[..., nq:2 * nq].reshape(bsz, s, GDN_HEADS, GDN_DK))
    gv = qkv[..., 2 * nq:].reshape(bsz, s, GDN_HEADS, GDN_DV)
    beta = jax.nn.sigmoid(gdn_b).reshape(bsz, s, 2, GDN_HEADS)
    glog = -jnp.exp(gdn_a_log.astype(f32)) * jax.nn.softplus(gdn_a.reshape(bsz, s, 2, GDN_HEADS) + gdn_dt_bias.astype(f32))
    o_f = gated_delta_chunked(gq, gk, gv, glog[:, :, 0], beta[:, :, 0])
    o_b = flip(gated_delta_chunked(flip(gq), flip(gk), flip(gv), flip(glog[:, :, 1]), flip(beta[:, :, 1])))
    o_gdn = rms_norm(o_f + o_b, gdn_norm_w) * jax.nn.silu(gdn_z.reshape(bsz, s, GDN_HEADS, GDN_DV))
    o_gdn = o_gdn.reshape(bsz, s, GDN_W)

    nqk = RET_HEADS * RET_DK
    rq = rotary(ret_qkvg[..., :nqk].reshape(bsz, s, RET_HEADS, RET_DK))
    rk = rotary(ret_qkvg[..., nqk:2 * nqk].reshape(bsz, s, RET_HEADS, RET_DK)) * RET_DK ** -0.5
    rv = ret_qkvg[..., 2 * nqk:2 * nqk + RET_W].reshape(bsz, s, RET_HEADS, RET_DV)
    rg = ret_qkvg[..., 2 * nqk + RET_W:]
    log_gamma = jnp.log1p(-jnp.exp2(-5.0 - jnp.arange(RET_HEADS, dtype=f32)))
    o_ret = retention_chunked(rq, rk, rv, log_gamma, True) + flip(retention_chunked(flip(rq), flip(rk), flip(rv), log_gamma, False))
    mu = jnp.mean(o_ret, -1, keepdims=True)
    var = jnp.mean(jnp.square(o_ret - mu), -1, keepdims=True)
    o_ret = ((o_ret - mu) * lax.rsqrt(var + LN_EPS)).reshape(bsz, s, RET_W) * ret_norm_g.astype(f32) + ret_norm_b.astype(f32)
    o_ret = jax.nn.silu(rg) * o_ret

    xbc = jax.nn.silu(depthwise_conv_centred(ssd_xbc, ssd_conv_w.astype(f32)) + ssd_conv_b.astype(f32))
    xs = xbc[..., :SSD_W].reshape(bsz, s, SSD_HEADS, SSD_P)
    bm = xbc[..., SSD_W:SSD_W + SSD_G * SSD_N].reshape(bsz, s, SSD_G, SSD_N)
    cm = xbc[..., SSD_W + SSD_G * SSD_N:].reshape(bsz, s, SSD_G, SSD_N)
    dt = jax.nn.softplus(ssd_dt.reshape(bsz, s, 2, SSD_HEADS) + ssd_dt_bias.astype(f32))
    a = -jnp.exp(ssd_a_log.astype(f32))
    y = ssd_chunked(xs, dt[:, :, 0], a[0], bm, cm) + flip(ssd_chunked(flip(xs), flip(dt[:, :, 1]), a[1], flip(bm), flip(cm)))
    y = y + ssd_d.astype(f32)[:, None] * xs
    y = y.reshape(bsz, s, SSD_W) * jax.nn.silu(ssd_z)
    y = rms_norm(y.reshape(bsz, s, SSD_G, SSD_W // SSD_G), ssd_norm_w.reshape(SSD_G, SSD_W // SSD_G)).reshape(bsz, s, SSD_W)

    mixed = jnp.concatenate([o_gdn, o_ret, y], -1).astype(h.dtype)
    return mixed @ w_out


def trunk(x, w_in, w_out, gdn_conv_w, gdn_a_log, gdn_dt_bias, gdn_norm_w, ret_norm_g, ret_norm_b,
          ssd_conv_w, ssd_conv_b, ssd_a_log, ssd_dt_bias, ssd_d, ssd_norm_w,
          ffn1_w_gate, ffn1_w_up, ffn1_w_down, ffn2_w_gate, ffn2_w_up, ffn2_w_down,
          ln1_g, ln1_b, ln2_g, ln2_b, ln3_g, ln3_b):
    for l in range(DEPTH):
        x = layer_norm(DEEPNORM_ALPHA * x + 0.5 * swiglu(x, ffn1_w_gate[l], ffn1_w_up[l], ffn1_w_down[l]), ln1_g[l], ln1_b[l])
        mix = hybrid_mixer(x, w_in[l], w_out[l], gdn_conv_w[l], gdn_a_log[l], gdn_dt_bias[l], gdn_norm_w[l],
                           ret_norm_g[l], ret_norm_b[l], ssd_conv_w[l], ssd_conv_b[l], ssd_a_log[l],
                           ssd_dt_bias[l], ssd_d[l], ssd_norm_w[l])
        x = layer_norm(DEEPNORM_ALPHA * x + mix, ln2_g[l], ln2_b[l])
        x = layer_norm(DEEPNORM_ALPHA * x + 0.5 * swiglu(x, ffn2_w_gate[l], ffn2_w_up[l], ffn2_w_down[l]), ln3_g[l], ln3_b[l])
    return x


def setup_inputs(seed: int = 0) -> dict:
    key = jax.random.key(seed)
    ks = jax.random.split(key, 32)
    f32 = jnp.float32

    def nrm(k, shape, scale):
        return jax.random.normal(k, shape, f32) * scale

    def gain(k, shape):
        return 1.0 + 0.02 * jax.random.normal(k, shape, f32)

    def a_log_init(k, shape):
        return jnp.log(jax.random.uniform(k, shape, f32, minval=1.0, maxval=16.0))

    def dt_bias_init(k, shape):
        dt = jnp.exp(jax.random.uniform(k, shape, f32, minval=math.log(1e-3), maxval=math.log(1e-1)))
        return dt + jnp.log(-jnp.expm1(-dt))

    return {
        'x_prompt': nrm(ks[0], (BATCH, SEQ, D_MODEL), 1.0),
        'x_sample': nrm(ks[1], (DEC_BATCH, DEC_SEQ, D_MODEL), 1.0),
        'w_in': nrm(ks[2], (DEPTH, D_MODEL, IN_W), D_MODEL ** -0.5),
        'w_out': nrm(ks[3], (DEPTH, MIX_W, D_MODEL), MIX_W ** -0.5 * DEEPNORM_BETA),
        'gdn_conv_w': nrm(ks[4], (DEPTH, CONV_K, GDN_QKV_W), CONV_K ** -0.5),
        'gdn_a_log': a_log_init(ks[5], (DEPTH, 2, GDN_HEADS)),
        'gdn_dt_bias': dt_bias_init(ks[6], (DEPTH, 2, GDN_HEADS)),
        'gdn_norm_w': gain(ks[7], (DEPTH, GDN_DV)),
        'ret_norm_g': gain(ks[8], (DEPTH, RET_W)),
        'ret_norm_b': nrm(ks[9], (DEPTH, RET_W), 0.02),
        'ssd_conv_w': nrm(ks[10], (DEPTH, CONV_K, SSD_CONV_W), CONV_K ** -0.5),
        'ssd_conv_b': nrm(ks[11], (DEPTH, SSD_CONV_W), 0.02),
        'ssd_a_log': a_log_init(ks[12], (DEPTH, 2, SSD_HEADS)),
        'ssd_dt_bias': dt_bias_init(ks[13], (DEPTH, 2, SSD_HEADS)),
        'ssd_d': gain(ks[14], (DEPTH, SSD_HEADS)),
        'ssd_norm_w': gain(ks[15], (DEPTH, SSD_W)),
        'ffn1_w_gate': nrm(ks[16], (DEPTH, D_MODEL, D_FF), D_MODEL ** -0.5),
        'ffn1_w_up': nrm(ks[17], (DEPTH, D_MODEL, D_FF), D_MODEL ** -0.5),
        'ffn1_w_down': nrm(ks[18], (DEPTH, D_FF, D_MODEL), D_FF ** -0.5 * DEEPNORM_BETA),
        'ffn2_w_gate': nrm(ks[19], (DEPTH, D_MODEL, D_FF), D_MODEL ** -0.5),
        'ffn2_w_up': nrm(ks[20], (DEPTH, D_MODEL, D_FF), D_MODEL ** -0.5),
        'ffn2_w_down': nrm(ks[21], (DEPTH, D_FF, D_MODEL), D_FF ** -0.5 * DEEPNORM_BETA),
        'ln1_g': gain(ks[22], (DEPTH, D_MODEL)),
        'ln1_b': nrm(ks[23], (DEPTH, D_MODEL), 0.02),
        'ln2_g': gain(ks[24], (DEPTH, D_MODEL)),
        'ln2_b': nrm(ks[25], (DEPTH, D_MODEL), 0.02),
        'ln3_g': gain(ks[26], (DEPTH, D_MODEL)),
        'ln3_b': nrm(ks[27], (DEPTH, D_MODEL), 0.02),
    }


def reference(x_prompt, x_sample, w_in, w_out, gdn_conv_w, gdn_a_log, gdn_dt_bias, gdn_norm_w, ret_norm_g, ret_norm_b,
              ssd_conv_w, ssd_conv_b, ssd_a_log, ssd_dt_bias, ssd_d, ssd_norm_w,
              ffn1_w_gate, ffn1_w_up, ffn1_w_down, ffn2_w_gate, ffn2_w_up, ffn2_w_down,
              ln1_g, ln1_b, ln2_g, ln2_b, ln3_g, ln3_b):
    y_prompt = trunk(x_prompt, w_in, w_out, gdn_conv_w, gdn_a_log, gdn_dt_bias, gdn_norm_w, ret_norm_g, ret_norm_b,
                     ssd_conv_w, ssd_conv_b, ssd_a_log, ssd_dt_bias, ssd_d, ssd_norm_w,
                     ffn1_w_gate, ffn1_w_up, ffn1_w_down, ffn2_w_gate, ffn2_w_up, ffn2_w_down,
                     ln1_g, ln1_b, ln2_g, ln2_b, ln3_g, ln3_b)
    y_sample = trunk(x_sample, w_in, w_out, gdn_conv_w, gdn_a_log, gdn_dt_bias, gdn_norm_w, ret_norm_g, ret_norm_b,
                     ssd_conv_w, ssd_conv_b, ssd_a_log, ssd_dt_bias, ssd_d, ssd_norm_w,
                     ffn1_w_gate, ffn1_w_up, ffn1_w_down, ffn2_w_gate, ffn2_w_up, ffn2_w_down,
                     ln1_g, ln1_b, ln2_g, ln2_b, ln3_g, ln3_b)
    return (y_prompt, y_sample)
```

```python
import functools
import math

import jax
import jax.numpy as jnp
import numpy as np
from jax import lax
from jax.experimental import pallas as pl
from jax.experimental.pallas import tpu as pltpu

F32 = jnp.float32
BF16 = jnp.bfloat16

D_MODEL = 2048
DEPTH = 2
CONV_K = 5
GDN_HEADS = 6
GDN_DK = 128
RET_HEADS = 4
RET_DK = 128
SSD_HEADS = 12
SSD_P = 64
SSD_N = 128
SSD_G = 2
D_FF = 5632
ROPE_BASE = 10000.0
LN_EPS = 1e-5
RMS_EPS = 1e-6
DEEPNORM_ALPHA = (2 * DEPTH) ** 0.25

GDN_W = GDN_HEADS * 128
RET_W = RET_HEADS * 128
SSD_W = SSD_HEADS * SSD_P
SSD_GROUP_W = SSD_W // SSD_G
SSD_PAIRS = SSD_HEADS // 2

LANES = 128
SUBLANES = 8
VMEM_LIMIT_BYTES = 60000 * 1024

CHUNK = 128
CONV_HALO = 8

CB_SSD_X = 0
CB_SSD_Z = 6
CB_GDN_Q = 12
CB_GDN_K = 18
CB_GDN_V = 24
CB_GDN_Z = 30
CB_RET_Q = 36
CB_RET_K = 40
CB_RET_V = 44
CB_RET_G = 48
CB_SSD_B = 52
CB_SSD_C = 54
CB_SMALL = 56
N_COL_BLOCKS = 57
IN_PROJ_COL_GROUPS = 3
COL_BLOCKS_PER_GROUP = N_COL_BLOCKS // IN_PROJ_COL_GROUPS
SMALL_GDN_LANES = 4 * GDN_HEADS
SMALL_SSD_LANE0 = 32
SMALL_SSD_LANES = 4 * SSD_PAIRS

_OFF_GDN_QKV = 0
_OFF_GDN_Z = 2304
_OFF_GDN_B = 3072
_OFF_GDN_A = 3084
_OFF_RET = 3096
_OFF_SSD_XBC = 5144
_OFF_SSD_Z = 6424
_OFF_SSD_DT = 7192

FFN_TOKEN_TILE = 512
FFN_FF_TILE = 512
PROJ_TOKEN_TILE = 512


def _cparams(semantics):
    return pltpu.CompilerParams(dimension_semantics=semantics, vmem_limit_bytes=VMEM_LIMIT_BYTES)


def _mm(a, b):
    return jnp.dot(a.astype(BF16), b.astype(BF16), preferred_element_type=F32)


def _mm_nt(a, b):
    return lax.dot_general(a.astype(BF16), b.astype(BF16), (((1,), (1,)), ((), ())), preferred_element_type=F32)


def _mm_tn(a, b):
    return lax.dot_general(a.astype(BF16), b.astype(BF16), (((0,), (0,)), ((), ())), preferred_element_type=F32)


def _mm_exact(a, b):
    return jnp.dot(a, b, precision=lax.Precision.HIGHEST, preferred_element_type=F32)


def _silu(x):
    return x * jax.nn.sigmoid(x)


def _softplus(x):
    return jnp.maximum(x, 0.0) + jnp.log1p(jnp.exp(-jnp.abs(x)))


def _layer_norm_rows(y, g, b):
    mu = jnp.mean(y, axis=-1, keepdims=True)
    yc = y - mu
    var = jnp.mean(yc * yc, axis=-1, keepdims=True)
    return yc * lax.rsqrt(var + LN_EPS) * g + b


def _ffn_kernel(x_ref, wg_ref, wu_ref, wd_ref, g_ref, b_ref, o_ref, xb_ref):
    j = pl.program_id(1)

    @pl.when(j == 0)
    def _():
        xb_ref[...] = x_ref[...].astype(BF16)

    xb = xb_ref[...]
    gate = jnp.dot(xb, wg_ref[...], preferred_element_type=F32)
    up = jnp.dot(xb, wu_ref[...], preferred_element_type=F32)
    h = (_silu(gate) * up).astype(BF16)
    part = jnp.dot(h, wd_ref[...], preferred_element_type=F32)

    @pl.when(j == 0)
    def _():
        o_ref[...] = part

    @pl.when(j > 0)
    def _():
        o_ref[...] += part

    @pl.when(j == pl.num_programs(1) - 1)
    def _():
        y = DEEPNORM_ALPHA * x_ref[...] + 0.5 * o_ref[...]
        o_ref[...] = _layer_norm_rows(y, g_ref[...], b_ref[...])


def _ffn(x, wg, wu, wd, g, b):
    t, d = x.shape
    f = wg.shape[1]
    tm, tf = FFN_TOKEN_TILE, FFN_FF_TILE
    return pl.pallas_call(
        _ffn_kernel,
        grid=(t // tm, f // tf),
        in_specs=[
            pl.BlockSpec((tm, d), lambda i, j: (i, 0)),
            pl.BlockSpec((d, tf), lambda i, j: (0, j)),
            pl.BlockSpec((d, tf), lambda i, j: (0, j)),
            pl.BlockSpec((tf, d), lambda i, j: (j, 0)),
            pl.BlockSpec((1, d), lambda i, j: (0, 0)),
            pl.BlockSpec((1, d), lambda i, j: (0, 0)),
        ],
        out_specs=pl.BlockSpec((tm, d), lambda i, j: (i, 0)),
        out_shape=jax.ShapeDtypeStruct((t, d), F32),
        scratch_shapes=[pltpu.VMEM((tm, d), BF16)],
        compiler_params=_cparams(("parallel", "arbitrary")),
        name="ffn",
    )(x, wg, wu, wd, g, b)


def _in_proj_kernel(x_ref, w_ref, o_ref):
    r = jnp.dot(x_ref[...].astype(BF16), w_ref[...], preferred_element_type=F32)
    for c in range(COL_BLOCKS_PER_GROUP):
        o_ref[c] = r[:, c * LANES:(c + 1) * LANES]


def _in_proj(x, w):
    t, d = x.shape
    tm = PROJ_TOKEN_TILE
    tn = COL_BLOCKS_PER_GROUP * LANES
    return pl.pallas_call(
        _in_proj_kernel,
        grid=(IN_PROJ_COL_GROUPS, t // tm),
        in_specs=[
            pl.BlockSpec((tm, d), lambda j, i: (i, 0)),
            pl.BlockSpec((d, tn), lambda j, i: (0, j)),
        ],
        out_specs=pl.BlockSpec((COL_BLOCKS_PER_GROUP, tm, LANES), lambda j, i: (j, i, 0)),
        out_shape=jax.ShapeDtypeStruct((N_COL_BLOCKS, t, LANES), F32),
        compiler_params=_cparams(("arbitrary", "arbitrary")),
        name="in_proj",
    )(x, w)


def _iota2(shape, axis):
    return lax.broadcasted_iota(jnp.int32, shape, axis)


def _tri_masks():
    i = _iota2((CHUNK, CHUNK), 0)
    j = _iota2((CHUNK, CHUNK), 1)
    return ((j <= i, j < i), (j >= i, j > i))


def _cumsum_mats():
    j = _iota2((CHUNK, CHUNK), 0)
    i = _iota2((CHUNK, CHUNK), 1)
    one = jnp.ones((CHUNK, CHUNK), F32)
    zero = jnp.zeros((CHUNK, CHUNK), F32)
    return jnp.where(j <= i, one, zero), jnp.where(j >= i, one, zero), one


def _conv_silu(x_ref, xp_ref, w_ref, bias, dst_ref, seq, post):
    n = seq // CHUNK
    zeros = jnp.zeros((CONV_HALO, LANES), F32)
    xp_ref[0:CONV_HALO, :] = zeros
    xp_ref[seq + CONV_HALO:seq + 2 * CONV_HALO, :] = zeros

    def copy(i, c):
        r0 = pl.multiple_of(i * CHUNK, CHUNK)
        xp_ref[pl.ds(r0 + CONV_HALO, CHUNK), :] = x_ref[pl.ds(r0, CHUNK), :]
        return c

    lax.fori_loop(0, n, copy, 0)
    w = w_ref[...]

    def conv(i, c):
        r0 = pl.multiple_of(i * CHUNK, CHUNK)
        acc = jnp.zeros((CHUNK, LANES), F32)
        for k in range(CONV_K):
            acc = acc + xp_ref[pl.ds(r0 + (CONV_HALO - CONV_K // 2 + k), CHUNK), :] * w[k:k + 1, :]
        if bias is not None:
            acc = acc + bias
        dst_ref[pl.ds(r0, CHUNK), :] = post(_silu(acc))
        return c

    lax.fori_loop(0, n, conv, 0)


def _l2norm_rows(t):
    return t * lax.rsqrt(jnp.sum(t * t, axis=-1, keepdims=True) + RMS_EPS)


def _small_transform(x, bias_vec, alog_vec):
    lane = _iota2(x.shape, 1)
    sp = _softplus(x + bias_vec)
    neg_a = -jnp.exp(alog_vec)
    is_gdn = lane < SMALL_GDN_LANES
    is_beta = jnp.logical_and(is_gdn, (lane & 3) < 2)
    gdn_vals = jnp.where(is_beta, jax.nn.sigmoid(x), neg_a * sp)
    vals = jnp.where(is_gdn, gdn_vals, sp)
    return vals, neg_a * sp


def _pad_rows_transpose(r16):
    padded = jnp.concatenate([r16, jnp.zeros((CHUNK - r16.shape[0], LANES), F32)], axis=0)
    return padded.T


def _unit_tri_inverse(lmat):
    i = _iota2((CHUNK, CHUNK), 0)
    j = _iota2((CHUNK, CHUNK), 1)

    def same(shift):
        return lax.shift_right_logical(i, shift) == lax.shift_right_logical(j, shift)

    eye = jnp.where(i == j, 1.0, 0.0).astype(F32)
    l16 = jnp.where(same(4), lmat, 0.0)
    x = eye - l16
    p = _mm(l16, l16)
    x = x + _mm(x, p)
    p = _mm(p, p)
    x = x + _mm(x, p)
    p = _mm(p, p)
    x = x + _mm(x, p)
    for shift in (5, 6, 7):
        off = jnp.where(jnp.logical_and(same(shift), jnp.logical_not(same(shift - 1))), lmat, 0.0)
        x = x - _mm(x, _mm(off, x))
    return x


def _gdn_prep(gram_kk, gram_qk, q, k, v, rt, r16, d, masks):
    incl, strict = masks[d]
    beta = rt[:, d:d + 1]
    gc_col = rt[:, 2 + d:3 + d]
    gc_row = r16[2 + d:3 + d, :]
    tot_col = rt[:, 10 + d:11 + d]
    tot_row = r16[10 + d:11 + d, :]
    decay = jnp.where(incl, jnp.exp(jnp.where(incl, gc_col - gc_row, 0.0)), 0.0)
    lmat = jnp.where(strict, gram_kk * beta * decay, 0.0)
    tinv = _unit_tri_inverse(lmat)
    eg = jnp.exp(gc_col)
    rhs = jnp.concatenate([v * beta, k * (beta * eg)], axis=1)
    sol = _mm(tinv, rhs)
    return dict(
        u=sol[:, :LANES],
        w=sol[:, LANES:],
        qk=gram_qk * decay,
        q_dec=q * eg,
        k_dec=k * jnp.exp(tot_col - gc_col),
        eg_last=jnp.exp(jnp.broadcast_to(tot_row, (CHUNK, LANES))),
    )


def _gdn_step(state, c):
    m = _mm(jnp.concatenate([c["w"], c["q_dec"]], axis=0), state)
    v_new = c["u"] - m[:CHUNK]
    out = m[CHUNK:] + _mm(c["qk"], v_new)
    state = state * c["eg_last"] + _mm_tn(c["k_dec"], v_new)
    return state, out


def _gdn_kernel(q_ref, k_ref, v_ref, z_ref, sm_ref, cwq_ref, cwk_ref, cwv_ref, bias_ref, alog_ref, nw_ref,
                o_ref, xp_ref, qn_ref, kn_ref, vn_ref, rrow_ref, rcol_ref, of_ref, ob_ref, *, seq):
    n = seq // CHUNK
    h = pl.program_id(1)

    _conv_silu(q_ref, xp_ref, cwq_ref, None, qn_ref, seq, lambda t: _l2norm_rows(t) * (GDN_DK ** -0.5))
    _conv_silu(k_ref, xp_ref, cwk_ref, None, kn_ref, seq, _l2norm_rows)
    _conv_silu(v_ref, xp_ref, cwv_ref, None, vn_ref, seq, lambda t: t)

    u_f, u_b, ones = _cumsum_mats()
    bias_vec = bias_ref[...]
    alog_vec = alog_ref[...]
    shift = (LANES - 4 * h) % LANES

    def gates(i, c):
        r0 = pl.multiple_of(i * CHUNK, CHUNK)
        vals, _ = _small_transform(sm_ref[pl.ds(r0, CHUNK), :], bias_vec, alog_vec)
        mine = pltpu.roll(vals, shift, axis=1)
        rows = mine.T[0:SUBLANES, :]
        row = _iota2((SUBLANES, LANES), 0)
        csum = jnp.where(row == 2, _mm_exact(rows, u_f), _mm_exact(rows, u_b))
        r16 = jnp.concatenate([jnp.where(row < 2, rows, csum), _mm_exact(rows, ones)], axis=0)
        rrow_ref[pl.ds(pl.multiple_of(i * 16, 16), 16), :] = r16
        rcol_ref[pl.ds(r0, CHUNK), :] = _pad_rows_transpose(r16)
        return c

    lax.fori_loop(0, n, gates, 0)

    masks = _tri_masks()

    def load_chunk(ci):
        r0 = pl.multiple_of(ci * CHUNK, CHUNK)
        q = qn_ref[pl.ds(r0, CHUNK), :]
        k = kn_ref[pl.ds(r0, CHUNK), :]
        v = vn_ref[pl.ds(r0, CHUNK), :]
        rt = rcol_ref[pl.ds(r0, CHUNK), :]
        r16 = rrow_ref[pl.ds(pl.multiple_of(ci * 16, 16), 16), :]
        grams = _mm_nt(jnp.concatenate([k, q], axis=0), k)
        return r0, grams[:CHUNK], grams[CHUNK:], q, k, v, rt, r16

    def sweep(i, carry):
        s_f, s_b = carry
        r0, gkk, gqk, q, k, v, rt, r16 = load_chunk(i)
        s_f, out = _gdn_step(s_f, _gdn_prep(gkk, gqk, q, k, v, rt, r16, 0, masks))
        of_ref[pl.ds(r0, CHUNK), :] = out
        r0, gkk, gqk, q, k, v, rt, r16 = load_chunk(n - 1 - i)
        s_b, out = _gdn_step(s_b, _gdn_prep(gkk, gqk, q, k, v, rt, r16, 1, masks))
        ob_ref[pl.ds(r0, CHUNK), :] = out
        return s_f, s_b

    zero_state = jnp.zeros((GDN_DK, LANES), F32)
    lax.fori_loop(0, n, sweep, (zero_state, zero_state))

    nw = nw_ref[...]

    def finish(i, c):
        r0 = pl.multiple_of(i * CHUNK, CHUNK)
        o = of_ref[pl.ds(r0, CHUNK), :] + ob_ref[pl.ds(r0, CHUNK), :]
        o = o * lax.rsqrt(jnp.mean(o * o, axis=-1, keepdims=True) + RMS_EPS) * nw
        o_ref[pl.ds(r0, CHUNK), :] = (o * _silu(z_ref[pl.ds(r0, CHUNK), :])).astype(o_ref.dtype)
        return c

    lax.fori_loop(0, n, finish, 0)


def _proj_block(cb0):
    return lambda seq: pl.BlockSpec((None, seq, LANES), lambda b, h: (cb0 + h, b, 0))


def _gdn(proj, conv_w, bias_vec, alog_vec, norm_w, bsz, seq):
    n = seq // CHUNK
    col = lambda c0: pl.BlockSpec((CONV_K, LANES), lambda b, h: (0, c0 + h))
    vec = pl.BlockSpec((1, LANES), lambda b, h: (0, 0))
    seq_f32 = pltpu.VMEM((seq, LANES), F32)
    return pl.pallas_call(
        functools.partial(_gdn_kernel, seq=seq),
        grid=(bsz, GDN_HEADS),
        in_specs=[
            _proj_block(CB_GDN_Q)(seq), _proj_block(CB_GDN_K)(seq), _proj_block(CB_GDN_V)(seq),
            _proj_block(CB_GDN_Z)(seq),
            pl.BlockSpec((None, seq, LANES), lambda b, h: (CB_SMALL, b, 0)),
            col(0), col(GDN_HEADS), col(2 * GDN_HEADS), vec, vec, vec,
        ],
        out_specs=pl.BlockSpec((seq, LANES), lambda b, h: (b, h)),
        out_shape=jax.ShapeDtypeStruct((bsz * seq, GDN_W), BF16),
        scratch_shapes=[
            pltpu.VMEM((seq + 2 * CONV_HALO, LANES), F32),
            seq_f32, seq_f32, seq_f32,
            pltpu.VMEM((n * 16, LANES), F32),
            seq_f32, seq_f32, seq_f32,
        ],
        compiler_params=_cparams(("parallel", "arbitrary")),
        name="gdn",
    )(proj, proj, proj, proj, proj, conv_w, conv_w, conv_w, bias_vec, alog_vec, norm_w)


def _ret_kernel(lg_ref, q_ref, k_ref, v_ref, g_ref, cos_ref, sin_ref, ng_ref, nb_ref,
                o_ref, of_ref, ob_ref, *, seq):
    n = seq // CHUNK
    lg = lg_ref[pl.program_id(1)]
    i = _iota2((CHUNK, CHUNK), 0).astype(F32)
    j = _iota2((CHUNK, CHUNK), 1).astype(F32)
    d_sym = jnp.exp(jnp.abs(i - j) * lg)
    q_dec_f = jnp.exp((i + 1.0) * lg)
    k_dec_f = jnp.exp((CHUNK - 1.0 - i) * lg)
    q_dec_b = jnp.exp((CHUNK - i) * lg)
    k_dec_b = jnp.exp(i * lg)
    chunk_decay = jnp.exp(jnp.full((RET_DK, LANES), CHUNK, F32) * lg)

    def rot(t, r0):
        return t * cos_ref[pl.ds(r0, CHUNK), :] + pltpu.roll(t, LANES // 2, axis=1) * sin_ref[pl.ds(r0, CHUNK), :]

    def load(ci):
        r0 = pl.multiple_of(ci * CHUNK, CHUNK)
        q = rot(q_ref[pl.ds(r0, CHUNK), :], r0)
        k = rot(k_ref[pl.ds(r0, CHUNK), :], r0) * (RET_DK ** -0.5)
        return r0, q, k, v_ref[pl.ds(r0, CHUNK), :]

    def sweep(c, carry):
        s_f, s_b = carry
        r0, q, k, v = load(c)
        of_ref[pl.ds(r0, CHUNK), :] = _mm(_mm_nt(q, k) * d_sym, v) + _mm(q * q_dec_f, s_f)
        s_f = s_f * chunk_decay + _mm_tn(k * k_dec_f, v)
        r0, q, k, v = load(n - 1 - c)
        ob_ref[pl.ds(r0, CHUNK), :] = _mm(q * q_dec_b, s_b)
        s_b = s_b * chunk_decay + _mm_tn(k * k_dec_b, v)
        return s_f, s_b

    zero_state = jnp.zeros((RET_DK, LANES), F32)
    lax.fori_loop(0, n, sweep, (zero_state, zero_state))

    ng = ng_ref[...]
    nb = nb_ref[...]

    def finish(c, carry):
        r0 = pl.multiple_of(c * CHUNK, CHUNK)
        o = of_ref[pl.ds(r0, CHUNK), :] + ob_ref[pl.ds(r0, CHUNK), :]
        mu = jnp.mean(o, axis=-1, keepdims=True)
        oc = o - mu
        var = jnp.mean(oc * oc, axis=-1, keepdims=True)
        o = oc * lax.rsqrt(var + LN_EPS) * ng + nb
        o_ref[pl.ds(r0, CHUNK), :] = (_silu(g_ref[pl.ds(r0, CHUNK), :]) * o).astype(o_ref.dtype)
        return carry

    lax.fori_loop(0, n, finish, 0)


def _ret(proj, log_gamma, cos_t, sin_t, norm_g, norm_b, bsz, seq):
    table = pl.BlockSpec((seq, LANES), lambda b, h: (0, 0))
    vec = pl.BlockSpec((1, LANES), lambda b, h: (0, h))
    seq_f32 = pltpu.VMEM((seq, LANES), F32)
    return pl.pallas_call(
        functools.partial(_ret_kernel, seq=seq),
        grid=(bsz, RET_HEADS),
        in_specs=[
            pl.BlockSpec(memory_space=pltpu.SMEM),
            _proj_block(CB_RET_Q)(seq), _proj_block(CB_RET_K)(seq), _proj_block(CB_RET_V)(seq),
            _proj_block(CB_RET_G)(seq),
            table, table, vec, vec,
        ],
        out_specs=pl.BlockSpec((seq, LANES), lambda b, h: (b, h)),
        out_shape=jax.ShapeDtypeStruct((bsz * seq, RET_W), BF16),
        scratch_shapes=[seq_f32, seq_f32],
        compiler_params=_cparams(("parallel", "arbitrary")),
        name="ret",
    )(log_gamma, proj, proj, proj, proj, cos_t, sin_t, norm_g, norm_b)


def _ssd_kernel(x_ref, z_ref, b_ref, c_ref, sm_ref, cwx_ref, cwb_ref, cwc_ref, cbx_ref, cbb_ref, cbc_ref,
                bias_ref, alog_ref, dvec_ref, o_ref, xp_ref, xs_ref, bm_ref, cm_ref, rrow_ref, rcol_ref,
                of_ref, ob_ref, *, seq):
    n = seq // CHUNK
    p = pl.program_id(1)
    ident = lambda t: t
    _conv_silu(x_ref, xp_ref, cwx_ref, cbx_ref[...], xs_ref, seq, ident)
    _conv_silu(b_ref, xp_ref, cwb_ref, cbb_ref[...], bm_ref, seq, ident)
    _conv_silu(c_ref, xp_ref, cwc_ref, cbc_ref[...], cm_ref, seq, ident)

    u_f, u_b, ones = _cumsum_mats()
    bias_vec = bias_ref[...]
    alog_vec = alog_ref[...]
    shift = (LANES - (SMALL_SSD_LANE0 + 4 * p)) % LANES

    def gates(i, c):
        r0 = pl.multiple_of(i * CHUNK, CHUNK)
        dt, la = _small_transform(sm_ref[pl.ds(r0, CHUNK), :], bias_vec, alog_vec)
        lane = _iota2((CHUNK, LANES), 1)
        mine = jnp.where(lane < 4, pltpu.roll(dt, shift, axis=1), pltpu.roll(la, (shift + 4) % LANES, axis=1))
        rows = mine.T[0:SUBLANES, :]
        row = _iota2((SUBLANES, LANES), 0)
        csum = jnp.where(row < 6, _mm_exact(rows, u_f), _mm_exact(rows, u_b))
        r16 = jnp.concatenate([jnp.where(row < 4, rows, csum), _mm_exact(rows, ones)], axis=0)
        rrow_ref[pl.ds(pl.multiple_of(i * 16, 16), 16), :] = r16
        rcol_ref[pl.ds(r0, CHUNK), :] = _pad_rows_transpose(r16)
        return c

    lax.fori_loop(0, n, gates, 0)

    masks = _tri_masks()
    lane = _iota2((CHUNK, LANES), 1)
    first = lane < SSD_P
    first_rows = _iota2((CHUNK, LANES), 0) < SSD_P

    def pick(col0, col1):
        return jnp.where(first, col0, col1)

    def step(ci, d, prev):
        r0 = pl.multiple_of(ci * CHUNK, CHUNK)
        x = xs_ref[pl.ds(r0, CHUNK), :]
        bm = bm_ref[pl.ds(r0, CHUNK), :]
        cm = cm_ref[pl.ds(r0, CHUNK), :]
        rt = rcol_ref[pl.ds(r0, CHUNK), :]
        r16 = rrow_ref[pl.ds(pl.multiple_of(ci * 16, 16), 16), :]
        incl = masks[d][0]
        xdt = x * pick(rt[:, 2 * d:2 * d + 1], rt[:, 2 * d + 1:2 * d + 2])
        cb = _mm_nt(cm, bm)
        lc = [rt[:, 4 + 2 * d + e:5 + 2 * d + e] for e in range(2)]
        tot = [rt[:, 12 + 2 * d + e:13 + 2 * d + e] for e in range(2)]
        y = jnp.zeros((CHUNK, LANES), F32)
        for e in range(2):
            lc_row = r16[4 + 2 * d + e:5 + 2 * d + e, :]
            seg = jnp.where(incl, jnp.exp(jnp.where(incl, lc[e] - lc_row, 0.0)), 0.0)
            head = first if e == 0 else jnp.logical_not(first)
            y = y + _mm(cb * seg, jnp.where(head, xdt, 0.0))
        y = y + _mm_nt(cm, prev) * pick(jnp.exp(lc[0]), jnp.exp(lc[1]))
        states = _mm_tn(xdt * pick(jnp.exp(tot[0] - lc[0]), jnp.exp(tot[1] - lc[1])), bm)
        tot_rows = [jnp.broadcast_to(r16[12 + 2 * d + e:13 + 2 * d + e, :], (CHUNK, LANES)) for e in range(2)]
        prev = prev * jnp.where(first_rows, jnp.exp(tot_rows[0]), jnp.exp(tot_rows[1])) + states
        return r0, y, prev

    def sweep(i, carry):
        s_f, s_b = carry
        r0, y, s_f = step(i, 0, s_f)
        of_ref[pl.ds(r0, CHUNK), :] = y
        r0, y, s_b = step(n - 1 - i, 1, s_b)
        ob_ref[pl.ds(r0, CHUNK), :] = y
        return s_f, s_b

    zero_state = jnp.zeros((2 * SSD_P, SSD_N), F32)
    lax.fori_loop(0, n, sweep, (zero_state, zero_state))

    dvec = dvec_ref[...]

    def finish(i, c):
        r0 = pl.multiple_of(i * CHUNK, CHUNK)
        y = of_ref[pl.ds(r0, CHUNK), :] + ob_ref[pl.ds(r0, CHUNK), :] + dvec * xs_ref[pl.ds(r0, CHUNK), :]
        o_ref[pl.ds(r0, CHUNK), :] = y * _silu(z_ref[pl.ds(r0, CHUNK), :])
        return c

    lax.fori_loop(0, n, finish, 0)


def _ssd(proj, conv_w, conv_b, bias_vec, alog_vec, dvec, bsz, seq):
    n = seq // CHUNK
    group = lambda p: p // (SSD_PAIRS // SSD_G)
    wcol = lambda fn: pl.BlockSpec((CONV_K, LANES), lambda b, p: (0, fn(p)))
    bcol = lambda fn: pl.BlockSpec((1, LANES), lambda b, p: (0, fn(p)))
    vec = pl.BlockSpec((1, LANES), lambda b, p: (0, 0))
    seq_f32 = pltpu.VMEM((seq, LANES), F32)
    x_col = lambda p: p
    b_col = lambda p: SSD_W // LANES + group(p)
    c_col = lambda p: SSD_W // LANES + SSD_G + group(p)
    return pl.pallas_call(
        functools.partial(_ssd_kernel, seq=seq),
        grid=(bsz, SSD_PAIRS),
        in_specs=[
            _proj_block(CB_SSD_X)(seq), _proj_block(CB_SSD_Z)(seq),
            pl.BlockSpec((None, seq, LANES), lambda b, p: (CB_SSD_B + group(p), b, 0)),
            pl.BlockSpec((None, seq, LANES), lambda b, p: (CB_SSD_C + group(p), b, 0)),
            pl.BlockSpec((None, seq, LANES), lambda b, p: (CB_SMALL, b, 0)),
            wcol(x_col), wcol(b_col), wcol(c_col), bcol(x_col), bcol(b_col), bcol(c_col),
            vec, vec, bcol(x_col),
        ],
        out_specs=pl.BlockSpec((seq, LANES), lambda b, p: (b, p)),
        out_shape=jax.ShapeDtypeStruct((bsz * seq, SSD_W), F32),
        scratch_shapes=[
            pltpu.VMEM((seq + 2 * CONV_HALO, LANES), F32),
            seq_f32, seq_f32, seq_f32,
            pltpu.VMEM((n * 16, LANES), F32),
            seq_f32, seq_f32, seq_f32,
        ],
        compiler_params=_cparams(("parallel", "arbitrary")),
        name="ssd",
    )(proj, proj, proj, proj, proj, conv_w, conv_w, conv_w, conv_b, conv_b, conv_b, bias_vec, alog_vec, dvec)


def _out_proj_kernel(x_ref, og_ref, or_ref, y_ref, wg_ref, wr_ref, wy_ref, nw_ref, g_ref, b_ref, o_ref):
    acc = jnp.dot(og_ref[...], wg_ref[...], preferred_element_type=F32)
    acc = acc + jnp.dot(or_ref[...], wr_ref[...], preferred_element_type=F32)
    y = y_ref[...]
    nw = nw_ref[...]
    for g in range(SSD_G):
        lo, hi = g * SSD_GROUP_W, (g + 1) * SSD_GROUP_W
        yg = y[:, lo:hi]
        yg = yg * lax.rsqrt(jnp.mean(yg * yg, axis=-1, keepdims=True) + RMS_EPS) * nw[:, lo:hi]
        acc = acc + jnp.dot(yg.astype(BF16), wy_ref[lo:hi, :], preferred_element_type=F32)
    o_ref[...] = _layer_norm_rows(DEEPNORM_ALPHA * x_ref[...] + acc, g_ref[...], b_ref[...])


def _out_proj(x, o_gdn, o_ret, y_ssd, w_gdn, w_ret, w_ssd, ssd_norm_w, g, b):
    t, d = x.shape
    tm = PROJ_TOKEN_TILE
    rows = lambda width: pl.BlockSpec((tm, width), lambda i: (i, 0))
    whole = lambda a: pl.BlockSpec(a.shape, lambda i: (0, 0))
    return pl.pallas_call(
        _out_proj_kernel,
        grid=(t // tm,),
        in_specs=[
            rows(d), rows(GDN_W), rows(RET_W), rows(SSD_W),
            whole(w_gdn), whole(w_ret), whole(w_ssd), whole(ssd_norm_w), whole(g), whole(b),
        ],
        out_specs=rows(d),
        out_shape=jax.ShapeDtypeStruct((t, d), F32),
        compiler_params=_cparams(("parallel",)),
        name="out_proj",
    )(x, o_gdn, o_ret, y_ssd, w_gdn, w_ret, w_ssd, ssd_norm_w, g, b)


def _small_lane_sources():
    col = -np.ones((LANES,), np.int64)
    par = -np.ones((LANES,), np.int64)
    for h in range(GDN_HEADS):
        for d in range(2):
            col[4 * h + d] = _OFF_GDN_B + d * GDN_HEADS + h
            col[4 * h + 2 + d] = _OFF_GDN_A + d * GDN_HEADS + h
            par[4 * h + 2 + d] = d * GDN_HEADS + h
    for p in range(SSD_PAIRS):
        for d in range(2):
            for e in range(2):
                lane = SMALL_SSD_LANE0 + 4 * p + 2 * d + e
                col[lane] = _OFF_SSD_DT + d * SSD_HEADS + 2 * p + e
                par[lane] = 2 * GDN_HEADS + d * SSD_HEADS + 2 * p + e
    return col, par


def _permute_w_in(w):
    col, _ = _small_lane_sources()
    small = jnp.where(jnp.asarray(col >= 0)[None, :], w[:, np.maximum(col, 0)], 0.0)
    xbc = _OFF_SSD_XBC
    return jnp.concatenate([
        w[:, xbc:xbc + SSD_W],
        w[:, _OFF_SSD_Z:_OFF_SSD_Z + SSD_W],
        w[:, _OFF_GDN_QKV:_OFF_GDN_B],
        w[:, _OFF_RET:_OFF_SSD_XBC],
        w[:, xbc + SSD_W:_OFF_SSD_Z],
        small,
    ], axis=1).astype(BF16)


def _small_lane_vector(gdn_param, ssd_param):
    _, par = _small_lane_sources()
    flat = jnp.concatenate([gdn_param.reshape(-1), ssd_param.reshape(-1)]).astype(F32)
    return jnp.where(jnp.asarray(par >= 0), flat[np.maximum(par, 0)], 0.0).reshape(1, LANES)


def _rotary_tables(seq):
    inv = ROPE_BASE ** (-jnp.arange(0, RET_DK, 2, dtype=F32) / RET_DK)
    ang = jnp.arange(seq, dtype=F32)[:, None] * inv[None, :]
    cos, sin = jnp.cos(ang), jnp.sin(ang)
    return jnp.concatenate([cos, cos], axis=1), jnp.concatenate([-sin, sin], axis=1)


def _trunk(x3, layers):
    bsz, seq, d = x3.shape
    x = x3.reshape(bsz * seq, d)
    cos_t, sin_t = _rotary_tables(seq)
    log_gamma = jnp.log1p(-jnp.exp2(-5.0 - jnp.arange(RET_HEADS, dtype=F32)))
    for p in layers:
        x = _ffn(x, *p["ffn1"], *p["ln1"])
        proj = _in_proj(x, p["w_in"])
        o_gdn = _gdn(proj, p["gdn_conv_w"], p["small_bias"], p["small_alog"], p["gdn_norm_w"], bsz, seq)
        o_ret = _ret(proj, log_gamma, cos_t, sin_t, p["ret_norm_g"], p["ret_norm_b"], bsz, seq)
        y_ssd = _ssd(proj, p["ssd_conv_w"], p["ssd_conv_b"], p["small_bias"], p["small_alog"], p["ssd_d"], bsz, seq)
        x = _out_proj(x, o_gdn, o_ret, y_ssd, *p["w_out"], p["ssd_norm_w"], *p["ln2"])
        x = _ffn(x, *p["ffn2"], *p["ln3"])
    return x.reshape(bsz, seq, d)


def kernel(x_prompt, x_sample, w_in, w_out, gdn_conv_w, gdn_a_log, gdn_dt_bias, gdn_norm_w, ret_norm_g, ret_norm_b,
           ssd_conv_w, ssd_conv_b, ssd_a_log, ssd_dt_bias, ssd_d, ssd_norm_w,
           ffn1_w_gate, ffn1_w_up, ffn1_w_down, ffn2_w_gate, ffn2_w_up, ffn2_w_down,
           ln1_g, ln1_b, ln2_g, ln2_b, ln3_g, ln3_b):
    row = lambda a: a.astype(F32).reshape(1, -1)
    layers = []
    for l in range(DEPTH):
        wo = w_out[l].astype(BF16)
        layers.append(dict(
            ffn1=(ffn1_w_gate[l].astype(BF16), ffn1_w_up[l].astype(BF16), ffn1_w_down[l].astype(BF16)),
            ffn2=(ffn2_w_gate[l].astype(BF16), ffn2_w_up[l].astype(BF16), ffn2_w_down[l].astype(BF16)),
            ln1=(row(ln1_g[l]), row(ln1_b[l])),
            ln2=(row(ln2_g[l]), row(ln2_b[l])),
            ln3=(row(ln3_g[l]), row(ln3_b[l])),
            w_in=_permute_w_in(w_in[l]),
            w_out=(wo[:GDN_W], wo[GDN_W:GDN_W + RET_W], wo[GDN_W + RET_W:]),
            gdn_conv_w=gdn_conv_w[l].astype(F32),
            small_bias=_small_lane_vector(gdn_dt_bias[l], ssd_dt_bias[l]),
            small_alog=_small_lane_vector(gdn_a_log[l], ssd_a_log[l]),
            gdn_norm_w=row(gdn_norm_w[l]),
            ret_norm_g=row(ret_norm_g[l]),
            ret_norm_b=row(ret_norm_b[l]),
            ssd_conv_w=ssd_conv_w[l].astype(F32),
            ssd_conv_b=row(ssd_conv_b[l]),
            ssd_d=jnp.repeat(ssd_d[l].astype(F32), SSD_P).reshape(1, SSD_W),
            ssd_norm_w=row(ssd_norm_w[l]),
        ))
    return _trunk(x_prompt, layers), _trunk(x_sample, layers)
```

```python
import functools
import math

import jax
import jax.numpy as jnp
import numpy as np
from jax import lax
from jax.experimental import pallas as pl
from jax.experimental.pallas import tpu as pltpu

F32 = jnp.float32
BF16 = jnp.bfloat16

D_MODEL = 2048
DEPTH = 2
CONV_K = 5
GDN_HEADS = 6
GDN_DK = 128
RET_HEADS = 4
RET_DK = 128
SSD_HEADS = 12
SSD_P = 64
SSD_N = 128
SSD_G = 2
D_FF = 5632
ROPE_BASE = 10000.0
LN_EPS = 1e-5
RMS_EPS = 1e-6
DEEPNORM_ALPHA = (2 * DEPTH) ** 0.25

GDN_W = GDN_HEADS * 128
RET_W = RET_HEADS * 128
SSD_W = SSD_HEADS * SSD_P
SSD_GROUP_W = SSD_W // SSD_G
SSD_PAIRS = SSD_HEADS // 2

LANES = 128
SUBLANES = 8
VMEM_LIMIT_BYTES = 60000 * 1024

CHUNK = 128
CONV_HALO = 8
LOCKSTEP_CHUNKS = 4

CB_SSD_X = 0
CB_SSD_Z = 6
CB_GDN_Q = 12
CB_GDN_K = 18
CB_GDN_V = 24
CB_GDN_Z = 30
CB_RET_Q = 36
CB_RET_K = 40
CB_RET_V = 44
CB_RET_G = 48
CB_SSD_B = 52
CB_SSD_C = 54
CB_SMALL = 56
N_COL_BLOCKS = 57
IN_PROJ_COL_GROUPS = 3
COL_BLOCKS_PER_GROUP = N_COL_BLOCKS // IN_PROJ_COL_GROUPS
SMALL_GDN_LANES = 4 * GDN_HEADS
SMALL_SSD_LANE0 = 32
SMALL_SSD_LANES = 4 * SSD_PAIRS

_OFF_GDN_QKV = 0
_OFF_GDN_Z = 2304
_OFF_GDN_B = 3072
_OFF_GDN_A = 3084
_OFF_RET = 3096
_OFF_SSD_XBC = 5144
_OFF_SSD_Z = 6424
_OFF_SSD_DT = 7192

FFN_TOKEN_TILE = 512
FFN_FF_TILE = 512
PROJ_TOKEN_TILE = 512


def _cparams(semantics):
    return pltpu.CompilerParams(dimension_semantics=semantics, vmem_limit_bytes=VMEM_LIMIT_BYTES)


def _mm(a, b):
    return jnp.dot(a.astype(BF16), b.astype(BF16), preferred_element_type=F32)


def _mm_nt(a, b):
    return lax.dot_general(a.astype(BF16), b.astype(BF16), (((1,), (1,)), ((), ())), preferred_element_type=F32)


def _mm_tn(a, b):
    return lax.dot_general(a.astype(BF16), b.astype(BF16), (((0,), (0,)), ((), ())), preferred_element_type=F32)


def _silu(x):
    return x * jax.nn.sigmoid(x)


def _softplus(x):
    return jnp.maximum(x, 0.0) + jnp.log1p(jnp.exp(-jnp.abs(x)))


def _layer_norm_rows(y, g, b):
    mu = jnp.mean(y, axis=-1, keepdims=True)
    yc = y - mu
    var = jnp.mean(yc * yc, axis=-1, keepdims=True)
    return yc * lax.rsqrt(var + LN_EPS) * g + b


def _ffn_kernel(x_ref, wg_ref, wu_ref, wd_ref, g_ref, b_ref, o_ref, xb_ref):
    j = pl.program_id(1)

    @pl.when(j == 0)
    def _():
        xb_ref[...] = x_ref[...].astype(BF16)
        o_ref[...] = jnp.zeros_like(o_ref)

    xb = xb_ref[...]
    gate = jnp.dot(xb, wg_ref[...], preferred_element_type=F32)
    up = jnp.dot(xb, wu_ref[...], preferred_element_type=F32)
    h = (_silu(gate) * up).astype(BF16)
    o_ref[...] += jnp.dot(h, wd_ref[...], preferred_element_type=F32)

    @pl.when(j == pl.num_programs(1) - 1)
    def _():
        y = DEEPNORM_ALPHA * x_ref[...] + 0.5 * o_ref[...]
        o_ref[...] = _layer_norm_rows(y, g_ref[...], b_ref[...])


def _ffn(x, wg, wu, wd, g, b):
    t, d = x.shape
    f = wg.shape[1]
    tm, tf = FFN_TOKEN_TILE, FFN_FF_TILE
    return pl.pallas_call(
        _ffn_kernel,
        grid=(t // tm, f // tf),
        in_specs=[
            pl.BlockSpec((tm, d), lambda i, j: (i, 0)),
            pl.BlockSpec((d, tf), lambda i, j: (0, j)),
            pl.BlockSpec((d, tf), lambda i, j: (0, j)),
            pl.BlockSpec((tf, d), lambda i, j: (j, 0)),
            pl.BlockSpec((1, d), lambda i, j: (0, 0)),
            pl.BlockSpec((1, d), lambda i, j: (0, 0)),
        ],
        out_specs=pl.BlockSpec((tm, d), lambda i, j: (i, 0)),
        out_shape=jax.ShapeDtypeStruct((t, d), F32),
        scratch_shapes=[pltpu.VMEM((tm, d), BF16)],
        compiler_params=_cparams(("parallel", "arbitrary")),
        name="ffn",
    )(x, wg, wu, wd, g, b)


def _in_proj_kernel(x_ref, w_ref, o_ref):
    r = jnp.dot(x_ref[...].astype(BF16), w_ref[...], preferred_element_type=F32)
    for c in range(COL_BLOCKS_PER_GROUP):
        o_ref[c] = r[:, c * LANES:(c + 1) * LANES]


def _in_proj(x, w):
    t, d = x.shape
    tm = PROJ_TOKEN_TILE
    tn = COL_BLOCKS_PER_GROUP * LANES
    return pl.pallas_call(
        _in_proj_kernel,
        grid=(IN_PROJ_COL_GROUPS, t // tm),
        in_specs=[
            pl.BlockSpec((tm, d), lambda j, i: (i, 0)),
            pl.BlockSpec((d, tn), lambda j, i: (0, j)),
        ],
        out_specs=pl.BlockSpec((COL_BLOCKS_PER_GROUP, tm, LANES), lambda j, i: (j, i, 0)),
        out_shape=jax.ShapeDtypeStruct((N_COL_BLOCKS, t, LANES), F32),
        compiler_params=_cparams(("arbitrary", "arbitrary")),
        name="in_proj",
    )(x, w)


def _lockstep(gens):
    results = [None] * len(gens)
    live = list(enumerate(gens))
    while live:
        still = []
        for idx, gen in live:
            try:
                next(gen)
                still.append((idx, gen))
            except StopIteration as stop:
                results[idx] = stop.value
        live = still
    return results


def _chunk_loop(n, chunk_gen):
    width = math.gcd(n, LOCKSTEP_CHUNKS)

    def trip(t, carry):
        _lockstep([chunk_gen(t * width + u) for u in range(width)])
        return carry

    lax.fori_loop(0, n // width, trip, 0)


def _rows(ci):
    return pl.ds(pl.multiple_of(ci * CHUNK, CHUNK), CHUNK)


def _iota2(shape, axis):
    return lax.broadcasted_iota(jnp.int32, shape, axis)


def _tri_masks():
    i = _iota2((CHUNK, CHUNK), 0)
    j = _iota2((CHUNK, CHUNK), 1)
    return ((j <= i, j < i), (j >= i, j > i))


def _cumsum_mat():
    j = _iota2((CHUNK, 3 * CHUNK), 0)
    i = _iota2((CHUNK, 3 * CHUNK), 1)
    prefix = jnp.logical_and(i < CHUNK, j <= i)
    suffix_or_total = jnp.logical_and(i >= CHUNK, jnp.logical_or(j + CHUNK >= i, i >= 2 * CHUNK))
    return jnp.where(jnp.logical_or(prefix, suffix_or_total), 1.0, 0.0).astype(BF16)


def _chunk_sums(rows_list, cmat):
    hi = [r.astype(BF16).astype(F32) for r in rows_list]
    mid = [(r - h).astype(BF16).astype(F32) for r, h in zip(rows_list, hi)]
    lo = [r - h - m for r, h, m in zip(rows_list, hi, mid)]
    sums = jnp.dot(jnp.concatenate(hi + mid + lo, axis=0).astype(BF16), cmat, preferred_element_type=F32)
    k = len(rows_list)
    out = []
    for u in range(k):
        s = sum(sums[(part * k + u) * SUBLANES:(part * k + u + 1) * SUBLANES] for part in range(3))
        out.append((s[:, :CHUNK], s[:, CHUNK:2 * CHUNK], s[:, 2 * CHUNK:]))
    return out


def _lane_vector_rows(vec, tmp_ref, row0):
    tmp_ref[...] = jnp.broadcast_to(vec, (CHUNK, LANES)).T
    return tmp_ref[pl.ds(row0, SUBLANES), :]


def _gate_rows_loop(n, sm_ref, tmp_ref, rrow_ref, rcol_ref, row0, row_values, assemble):
    width = math.gcd(n, LOCKSTEP_CHUNKS)
    cmat = _cumsum_mat()

    def trip(t, carry):
        chunks = [t * width + u for u in range(width)]
        for u, ci in enumerate(chunks):
            tmp_ref[u] = sm_ref[_rows(ci), :].T
        rows = [row_values(tmp_ref[u, pl.ds(row0, SUBLANES), :]) for u in range(width)]
        sums = _chunk_sums(rows, cmat)
        r16s = [assemble(r, *s) for r, s in zip(rows, sums)]
        for ci, r16 in zip(chunks, r16s):
            rrow_ref[pl.ds(pl.multiple_of(ci * 16, 16), 16), :] = r16
        for ci, r16 in zip(chunks, r16s):
            rcol_ref[_rows(ci), :] = _pad_rows_transpose(r16)
        return carry

    lax.fori_loop(0, n // width, trip, 0)


def _conv_silu(x_ref, xp_ref, w_ref, bias, dst_ref, seq, l2_scale=None):
    n = seq // CHUNK
    zeros = jnp.zeros((CONV_HALO, LANES), F32)
    xp_ref[0:CONV_HALO, :] = zeros
    xp_ref[seq + CONV_HALO:seq + 2 * CONV_HALO, :] = zeros

    def copy(ci):
        r0 = pl.multiple_of(ci * CHUNK, CHUNK)
        xp_ref[pl.ds(r0 + CONV_HALO, CHUNK), :] = x_ref[pl.ds(r0, CHUNK), :]
        return
        yield

    _chunk_loop(n, copy)
    w = w_ref[...]

    def conv(ci):
        r0 = pl.multiple_of(ci * CHUNK, CHUNK)
        acc = jnp.zeros((CHUNK, LANES), F32)
        for k in range(CONV_K):
            acc = acc + xp_ref[pl.ds(r0 + (CONV_HALO - CONV_K // 2 + k), CHUNK), :] * w[k:k + 1, :]
        if bias is not None:
            acc = acc + bias
        t = _silu(acc)
        if l2_scale is not None:
            ss = jnp.sum(t * t, axis=-1, keepdims=True)
            yield
            t = t * (lax.rsqrt(ss + RMS_EPS) * l2_scale)
        dst_ref[pl.ds(r0, CHUNK), :] = t
        return
        yield

    _chunk_loop(n, conv)


def _pad_rows_transpose(r16):
    padded = jnp.concatenate([r16, jnp.zeros((CHUNK - r16.shape[0], LANES), F32)], axis=0)
    return padded.T


def _unit_tri_inverse(lmat):
    i = _iota2((CHUNK, CHUNK), 0)
    j = _iota2((CHUNK, CHUNK), 1)

    def same(shift):
        return lax.shift_right_logical(i, shift) == lax.shift_right_logical(j, shift)

    eye = jnp.where(i == j, 1.0, 0.0).astype(F32)
    l16 = jnp.where(same(4), lmat, 0.0)
    x = eye - l16
    p = _mm(l16, l16)
    yield
    for _ in range(2):
        x, p = x + _mm(x, p), _mm(p, p)
        yield
    x = x + _mm(x, p)
    yield
    for shift in (5, 6, 7):
        off = jnp.where(jnp.logical_and(same(shift), jnp.logical_not(same(shift - 1))), lmat, 0.0)
        ex = _mm(off, x)
        yield
        x = x - _mm(x, ex)
        yield
    return x


def _gdn_chunk_terms(gram_kk, gram_qk, q, k, v, rt, r16, d, masks):
    incl, strict = masks[d]
    beta = rt[:, d:d + 1]
    gc_col = rt[:, 2 + d:3 + d]
    gc_row = r16[2 + d:3 + d, :]
    tot_col = rt[:, 10 + d:11 + d]
    decay = jnp.where(incl, jnp.exp(jnp.where(incl, gc_col - gc_row, 0.0)), 0.0)
    lmat = jnp.where(strict, gram_kk * beta * decay, 0.0)
    tinv = yield from _unit_tri_inverse(lmat)
    eg = jnp.exp(gc_col)
    sol = _mm(tinv, jnp.concatenate([v * beta, k * (beta * eg)], axis=1)).astype(BF16)
    yield
    kt = _mm_tn(k * jnp.exp(tot_col - gc_col), sol)
    qs = _mm(gram_qk * decay, sol)
    yield
    return kt[:, LANES:], kt[:, :LANES], q * eg - qs[:, LANES:], qs[:, :LANES]


def _gdn_kernel(q_ref, k_ref, v_ref, z_ref, sm_ref, cwq_ref, cwk_ref, cwv_ref, bias_ref, alog_ref, nw_ref,
                o_ref, xp_ref, qn_ref, kn_ref, vn_ref, rrow_ref, rcol_ref, of_ref, ob_ref, mq_ref, nn_ref,
                tmp_ref, *, seq):
    n = seq // CHUNK
    h = pl.program_id(1)

    _conv_silu(q_ref, xp_ref, cwq_ref, None, qn_ref, seq, l2_scale=GDN_DK ** -0.5)
    _conv_silu(k_ref, xp_ref, cwk_ref, None, kn_ref, seq, l2_scale=1.0)
    _conv_silu(v_ref, xp_ref, cwv_ref, None, vn_ref, seq)

    row0 = 4 * h
    row = _iota2((SUBLANES, LANES), 0)
    bias8 = _lane_vector_rows(bias_ref[...], tmp_ref.at[0], row0)
    neg_a8 = -jnp.exp(_lane_vector_rows(alog_ref[...], tmp_ref.at[0], row0))

    def gate_values(raw8):
        return jnp.where(row < 2, jax.nn.sigmoid(raw8), neg_a8 * _softplus(raw8 + bias8))

    def gate_rows(vals, prefix, suffix, total):
        return jnp.concatenate([jnp.where(row < 2, vals, jnp.where(row == 2, prefix, suffix)), total], axis=0)

    _gate_rows_loop(n, sm_ref, tmp_ref, rrow_ref, rcol_ref, row0, gate_values, gate_rows)

    masks = _tri_masks()
    outs = (of_ref, ob_ref)

    def local_terms(ci):
        q = qn_ref[_rows(ci), :]
        k = kn_ref[_rows(ci), :]
        v = vn_ref[_rows(ci), :]
        rt = rcol_ref[_rows(ci), :]
        r16 = rrow_ref[pl.ds(pl.multiple_of(ci * 16, 16), 16), :]
        grams = _mm_nt(jnp.concatenate([k, q], axis=0), k)
        yield
        terms = yield from _lockstep_gen(
            [_gdn_chunk_terms(grams[:CHUNK], grams[CHUNK:], q, k, v, rt, r16, d, masks) for d in range(2)])
        for d, (m_, n_, q_, o_) in enumerate(terms):
            mq_ref[d, pl.ds(pl.multiple_of(2 * ci * CHUNK, CHUNK), 2 * CHUNK), :] = (
                jnp.concatenate([m_, q_], axis=0).astype(BF16))
            nn_ref[d, _rows(ci), :] = n_
            outs[d][_rows(ci), :] = o_

    _chunk_loop(n, local_terms)

    def advance(d, ci, state):
        m = jnp.dot(mq_ref[d, pl.ds(pl.multiple_of(2 * ci * CHUNK, CHUNK), 2 * CHUNK), :], state.astype(BF16),
                    preferred_element_type=F32)
        yield
        outs[d][_rows(ci), :] += m[CHUNK:]
        tot_row = rrow_ref[pl.ds(ci * 16 + 10 + d, 1), :]
        return state * jnp.exp(jnp.broadcast_to(tot_row, (GDN_DK, LANES))) - m[:CHUNK] + nn_ref[d, _rows(ci), :]

    def sweep(i, carry):
        return tuple(_lockstep([advance(0, i, carry[0]), advance(1, n - 1 - i, carry[1])]))

    zero_state = jnp.zeros((GDN_DK, LANES), F32)
    lax.fori_loop(0, n, sweep, (zero_state, zero_state))

    nw = nw_ref[...]

    def finish(ci):
        o = of_ref[_rows(ci), :] + ob_ref[_rows(ci), :]
        ms = jnp.mean(o * o, axis=-1, keepdims=True)
        yield
        o = o * (lax.rsqrt(ms + RMS_EPS) * nw)
        o_ref[_rows(ci), :] = (o * _silu(z_ref[_rows(ci), :])).astype(o_ref.dtype)

    _chunk_loop(n, finish)


def _lockstep_gen(gens):
    results = [None] * len(gens)
    live = list(enumerate(gens))
    while live:
        still = []
        for idx, gen in live:
            try:
                next(gen)
                still.append((idx, gen))
            except StopIteration as stop:
                results[idx] = stop.value
        live = still
        if live:
            yield
    return results


def _proj_block(cb0):
    return lambda seq: pl.BlockSpec((None, seq, LANES), lambda b, h: (cb0 + h, b, 0))


def _gdn(proj, conv_w, bias_vec, alog_vec, norm_w, bsz, seq):
    n = seq // CHUNK
    col = lambda c0: pl.BlockSpec((CONV_K, LANES), lambda b, h: (0, c0 + h))
    vec = pl.BlockSpec((1, LANES), lambda b, h: (0, 0))
    seq_f32 = pltpu.VMEM((seq, LANES), F32)
    return pl.pallas_call(
        functools.partial(_gdn_kernel, seq=seq),
        grid=(bsz, GDN_HEADS),
        in_specs=[
            _proj_block(CB_GDN_Q)(seq), _proj_block(CB_GDN_K)(seq), _proj_block(CB_GDN_V)(seq),
            _proj_block(CB_GDN_Z)(seq),
            pl.BlockSpec((None, seq, LANES), lambda b, h: (CB_SMALL, b, 0)),
            col(0), col(GDN_HEADS), col(2 * GDN_HEADS), vec, vec, vec,
        ],
        out_specs=pl.BlockSpec((seq, LANES), lambda b, h: (b, h)),
        out_shape=jax.ShapeDtypeStruct((bsz * seq, GDN_W), BF16),
        scratch_shapes=[
            pltpu.VMEM((seq + 2 * CONV_HALO, LANES), F32),
            seq_f32, seq_f32, seq_f32,
            pltpu.VMEM((n * 16, LANES), F32),
            seq_f32, seq_f32, seq_f32,
            pltpu.VMEM((2, 2 * seq, LANES), BF16),
            pltpu.VMEM((2, seq, LANES), F32),
            pltpu.VMEM((LOCKSTEP_CHUNKS, CHUNK, LANES), F32),
        ],
        compiler_params=_cparams(("parallel", "arbitrary")),
        name="gdn",
    )(proj, proj, proj, proj, proj, conv_w, conv_w, conv_w, bias_vec, alog_vec, norm_w)


def _ret_kernel(lg_ref, q_ref, k_ref, v_ref, g_ref, cos_ref, sin_ref, ng_ref, nb_ref,
                o_ref, of_ref, ob_ref, *, seq):
    n = seq // CHUNK
    width = math.gcd(n, LOCKSTEP_CHUNKS)
    lg = lg_ref[pl.program_id(1)]
    i = _iota2((CHUNK, CHUNK), 0).astype(F32)
    j = _iota2((CHUNK, CHUNK), 1).astype(F32)
    d_sym = jnp.exp(jnp.abs(i - j) * lg)
    q_dec = (jnp.exp((i + 1.0) * lg), jnp.exp((CHUNK - i) * lg))
    k_dec = (jnp.exp((CHUNK - 1.0 - i) * lg), jnp.exp(i * lg))
    chunk_decay = jnp.exp(jnp.full((RET_DK, LANES), CHUNK, F32) * lg)
    outs = (of_ref, ob_ref)

    def rot(t, ci):
        return t * cos_ref[_rows(ci), :] + pltpu.roll(t, LANES // 2, axis=1) * sin_ref[_rows(ci), :]

    def chunk_terms(d, ci):
        q = rot(q_ref[_rows(ci), :], ci)
        k = rot(k_ref[_rows(ci), :], ci) * (RET_DK ** -0.5)
        v = v_ref[_rows(ci), :]
        kv = _mm_tn(k * k_dec[d], v)
        scores = _mm_nt(q, k) * d_sym if d == 0 else None
        yield
        intra = _mm(scores, v) if d == 0 else None
        return q, kv, intra

    def sweep(t, carry):
        chunks = ([t * width + u for u in range(width)], [n - 1 - (t * width + u) for u in range(width)])
        terms = _lockstep([chunk_terms(d, ci) for d in range(2) for ci in chunks[d]])
        states = list(carry)
        for d in range(2):
            for u, ci in enumerate(chunks[d]):
                q, kv, intra = terms[d * width + u]
                inter = _mm(q * q_dec[d], states[d])
                outs[d][_rows(ci), :] = inter if intra is None else intra + inter
                states[d] = states[d] * chunk_decay + kv
        return tuple(states)

    zero_state = jnp.zeros((RET_DK, LANES), F32)
    lax.fori_loop(0, n // width, sweep, (zero_state, zero_state))

    ng = ng_ref[...]
    nb = nb_ref[...]

    def finish(ci):
        o = of_ref[_rows(ci), :] + ob_ref[_rows(ci), :]
        mu = jnp.mean(o, axis=-1, keepdims=True)
        yield
        oc = o - mu
        var = jnp.mean(oc * oc, axis=-1, keepdims=True)
        yield
        o = oc * (lax.rsqrt(var + LN_EPS) * ng) + nb
        o_ref[_rows(ci), :] = (_silu(g_ref[_rows(ci), :]) * o).astype(o_ref.dtype)

    _chunk_loop(n, finish)


def _ret(proj, log_gamma, cos_t, sin_t, norm_g, norm_b, bsz, seq):
    table = pl.BlockSpec((seq, LANES), lambda b, h: (0, 0))
    vec = pl.BlockSpec((1, LANES), lambda b, h: (0, h))
    seq_f32 = pltpu.VMEM((seq, LANES), F32)
    return pl.pallas_call(
        functools.partial(_ret_kernel, seq=seq),
        grid=(bsz, RET_HEADS),
        in_specs=[
            pl.BlockSpec(memory_space=pltpu.SMEM),
            _proj_block(CB_RET_Q)(seq), _proj_block(CB_RET_K)(seq), _proj_block(CB_RET_V)(seq),
            _proj_block(CB_RET_G)(seq),
            table, table, vec, vec,
        ],
        out_specs=pl.BlockSpec((seq, LANES), lambda b, h: (b, h)),
        out_shape=jax.ShapeDtypeStruct((bsz * seq, RET_W), BF16),
        scratch_shapes=[seq_f32, seq_f32],
        compiler_params=_cparams(("parallel", "arbitrary")),
        name="ret",
    )(log_gamma, proj, proj, proj, proj, cos_t, sin_t, norm_g, norm_b)


def _ssd_kernel(x_ref, z_ref, b_ref, c_ref, sm_ref, cwx_ref, cwb_ref, cwc_ref, cbx_ref, cbb_ref, cbc_ref,
                bias_ref, alog_ref, dvec_ref, o_ref, xp_ref, xs_ref, bm_ref, cm_ref, rrow_ref, rcol_ref,
                of_ref, ob_ref, st_ref, sc_ref, tmp_ref, *, seq):
    n = seq // CHUNK
    width = math.gcd(n, LOCKSTEP_CHUNKS)
    p = pl.program_id(1)
    _conv_silu(x_ref, xp_ref, cwx_ref, cbx_ref[...], xs_ref, seq)
    _conv_silu(b_ref, xp_ref, cwb_ref, cbb_ref[...], bm_ref, seq)
    _conv_silu(c_ref, xp_ref, cwc_ref, cbc_ref[...], cm_ref, seq)

    row0 = SMALL_SSD_LANE0 + 4 * p
    row = _iota2((SUBLANES, LANES), 0)

    def rows_twice(x8):
        return jnp.where(row < 4, x8, pltpu.roll(x8, 4, axis=0))

    bias8 = _lane_vector_rows(bias_ref[...], tmp_ref.at[0], row0)
    neg_a8 = rows_twice(-jnp.exp(_lane_vector_rows(alog_ref[...], tmp_ref.at[0], row0)))

    def gate_values(raw8):
        dt = rows_twice(_softplus(raw8 + bias8))
        return jnp.where(row < 4, dt, dt * neg_a8)

    def gate_rows(vals, prefix, suffix, total):
        return jnp.concatenate([jnp.where(row < 4, vals, jnp.where(row < 6, prefix, suffix)), total], axis=0)

    _gate_rows_loop(n, sm_ref, tmp_ref, rrow_ref, rcol_ref, row0, gate_values, gate_rows)

    masks = _tri_masks()
    first = _iota2((CHUNK, LANES), 1) < SSD_P
    first_rows = _iota2((CHUNK, LANES), 0) < SSD_P
    outs = (of_ref, ob_ref)

    def col(rt, c):
        return jnp.broadcast_to(rt[:, c:c + 1], (CHUNK, LANES))

    def total_rows(ci, d):
        return [jnp.broadcast_to(rrow_ref[pl.ds(ci * 16 + 12 + 2 * d + e, 1), :], (CHUNK, LANES)) for e in range(2)]

    def local_terms(ci):
        x = xs_ref[_rows(ci), :]
        bm = bm_ref[_rows(ci), :]
        cm = cm_ref[_rows(ci), :]
        rt = rcol_ref[_rows(ci), :]
        r16 = rrow_ref[pl.ds(pl.multiple_of(ci * 16, 16), 16), :]
        cb = _mm_nt(cm, bm)
        xdts, lc_sels = [], []
        for d in range(2):
            lc = [col(rt, 4 + 2 * d + e) for e in range(2)]
            tot = total_rows(ci, d)
            xdt = x * jnp.where(first, col(rt, 2 * d), col(rt, 2 * d + 1))
            lc_sel = jnp.where(first, lc[0], lc[1])
            sc_ref[d, _rows(ci), :] = jnp.exp(lc_sel)
            st_ref[d, _rows(ci), :] = _mm_tn(xdt * jnp.exp(jnp.where(first, tot[0], tot[1]) - lc_sel), bm)
            xdts.append(xdt)
            lc_sels.append(lc)
        yield
        for d in range(2):
            incl = masks[d][0]
            y = jnp.zeros((CHUNK, LANES), F32)
            for e in range(2):
                lc_row = r16[4 + 2 * d + e:5 + 2 * d + e, :]
                seg = jnp.where(incl, jnp.exp(jnp.where(incl, lc_sels[d][e] - lc_row, 0.0)), 0.0)
                head = first if e == 0 else jnp.logical_not(first)
                y = y + _mm(cb * seg, jnp.where(head, xdts[d], 0.0))
            outs[d][_rows(ci), :] = y

    _chunk_loop(n, local_terms)

    def sweep(t, carry):
        chunks = ([t * width + u for u in range(width)], [n - 1 - (t * width + u) for u in range(width)])
        states = list(carry)
        for d in range(2):
            for ci in chunks[d]:
                outs[d][_rows(ci), :] += _mm_nt(cm_ref[_rows(ci), :], states[d]) * sc_ref[d, _rows(ci), :]
                tot = total_rows(ci, d)
                states[d] = states[d] * jnp.exp(jnp.where(first_rows, tot[0], tot[1])) + st_ref[d, _rows(ci), :]
        return tuple(states)

    zero_state = jnp.zeros((2 * SSD_P, SSD_N), F32)
    lax.fori_loop(0, n // width, sweep, (zero_state, zero_state))

    dvec = dvec_ref[...]

    def finish(ci):
        y = of_ref[_rows(ci), :] + ob_ref[_rows(ci), :] + dvec * xs_ref[_rows(ci), :]
        o_ref[_rows(ci), :] = y * _silu(z_ref[_rows(ci), :])
        return
        yield

    _chunk_loop(n, finish)


def _ssd(proj, conv_w, conv_b, bias_vec, alog_vec, dvec, bsz, seq):
    n = seq // CHUNK
    group = lambda p: p // (SSD_PAIRS // SSD_G)
    wcol = lambda fn: pl.BlockSpec((CONV_K, LANES), lambda b, p: (0, fn(p)))
    bcol = lambda fn: pl.BlockSpec((1, LANES), lambda b, p: (0, fn(p)))
    vec = pl.BlockSpec((1, LANES), lambda b, p: (0, 0))
    seq_f32 = pltpu.VMEM((seq, LANES), F32)
    x_col = lambda p: p
    b_col = lambda p: SSD_W // LANES + group(p)
    c_col = lambda p: SSD_W // LANES + SSD_G + group(p)
    return pl.pallas_call(
        functools.partial(_ssd_kernel, seq=seq),
        grid=(bsz, SSD_PAIRS),
        in_specs=[
            _proj_block(CB_SSD_X)(seq), _proj_block(CB_SSD_Z)(seq),
            pl.BlockSpec((None, seq, LANES), lambda b, p: (CB_SSD_B + group(p), b, 0)),
            pl.BlockSpec((None, seq, LANES), lambda b, p: (CB_SSD_C + group(p), b, 0)),
            pl.BlockSpec((None, seq, LANES), lambda b, p: (CB_SMALL, b, 0)),
            wcol(x_col), wcol(b_col), wcol(c_col), bcol(x_col), bcol(b_col), bcol(c_col),
            vec, vec, bcol(x_col),
        ],
        out_specs=pl.BlockSpec((seq, LANES), lambda b, p: (b, p)),
        out_shape=jax.ShapeDtypeStruct((bsz * seq, SSD_W), F32),
        scratch_shapes=[
            pltpu.VMEM((seq + 2 * CONV_HALO, LANES), F32),
            seq_f32, seq_f32, seq_f32,
            pltpu.VMEM((n * 16, LANES), F32),
            seq_f32, seq_f32, seq_f32,
            pltpu.VMEM((2, seq, LANES), F32),
            pltpu.VMEM((2, seq, LANES), F32),
            pltpu.VMEM((LOCKSTEP_CHUNKS, CHUNK, LANES), F32),
        ],
        compiler_params=_cparams(("parallel", "arbitrary")),
        name="ssd",
    )(proj, proj, proj, proj, proj, conv_w, conv_w, conv_w, conv_b, conv_b, conv_b, bias_vec, alog_vec, dvec)


def _out_proj_kernel(x_ref, og_ref, or_ref, y_ref, wg_ref, wr_ref, wy_ref, nw_ref, g_ref, b_ref, o_ref):
    acc = jnp.dot(og_ref[...], wg_ref[...], preferred_element_type=F32)
    acc = acc + jnp.dot(or_ref[...], wr_ref[...], preferred_element_type=F32)
    y = y_ref[...]
    nw = nw_ref[...]
    for g in range(SSD_G):
        lo, hi = g * SSD_GROUP_W, (g + 1) * SSD_GROUP_W
        yg = y[:, lo:hi]
        yg = yg * lax.rsqrt(jnp.mean(yg * yg, axis=-1, keepdims=True) + RMS_EPS) * nw[:, lo:hi]
        acc = acc + jnp.dot(yg.astype(BF16), wy_ref[lo:hi, :], preferred_element_type=F32)
    o_ref[...] = _layer_norm_rows(DEEPNORM_ALPHA * x_ref[...] + acc, g_ref[...], b_ref[...])


def _out_proj(x, o_gdn, o_ret, y_ssd, w_gdn, w_ret, w_ssd, ssd_norm_w, g, b):
    t, d = x.shape
    tm = PROJ_TOKEN_TILE
    rows = lambda width: pl.BlockSpec((tm, width), lambda i: (i, 0))
    whole = lambda a: pl.BlockSpec(a.shape, lambda i: (0, 0))
    return pl.pallas_call(
        _out_proj_kernel,
        grid=(t // tm,),
        in_specs=[
            rows(d), rows(GDN_W), rows(RET_W), rows(SSD_W),
            whole(w_gdn), whole(w_ret), whole(w_ssd), whole(ssd_norm_w), whole(g), whole(b),
        ],
        out_specs=rows(d),
        out_shape=jax.ShapeDtypeStruct((t, d), F32),
        compiler_params=_cparams(("parallel",)),
        name="out_proj",
    )(x, o_gdn, o_ret, y_ssd, w_gdn, w_ret, w_ssd, ssd_norm_w, g, b)


def _small_lane_sources():
    col = -np.ones((LANES,), np.int64)
    par = -np.ones((LANES,), np.int64)
    for h in range(GDN_HEADS):
        for d in range(2):
            col[4 * h + d] = _OFF_GDN_B + d * GDN_HEADS + h
            col[4 * h + 2 + d] = _OFF_GDN_A + d * GDN_HEADS + h
            par[4 * h + 2 + d] = d * GDN_HEADS + h
    for p in range(SSD_PAIRS):
        for d in range(2):
            for e in range(2):
                lane = SMALL_SSD_LANE0 + 4 * p + 2 * d + e
                col[lane] = _OFF_SSD_DT + d * SSD_HEADS + 2 * p + e
                par[lane] = 2 * GDN_HEADS + d * SSD_HEADS + 2 * p + e
    return col, par


def _permute_w_in(w):
    col, _ = _small_lane_sources()
    small = jnp.where(jnp.asarray(col >= 0)[None, :], w[:, np.maximum(col, 0)], 0.0)
    xbc = _OFF_SSD_XBC
    return jnp.concatenate([
        w[:, xbc:xbc + SSD_W],
        w[:, _OFF_SSD_Z:_OFF_SSD_Z + SSD_W],
        w[:, _OFF_GDN_QKV:_OFF_GDN_B],
        w[:, _OFF_RET:_OFF_SSD_XBC],
        w[:, xbc + SSD_W:_OFF_SSD_Z],
        small,
    ], axis=1).astype(BF16)


def _small_lane_vector(gdn_param, ssd_param):
    _, par = _small_lane_sources()
    flat = jnp.concatenate([gdn_param.reshape(-1), ssd_param.reshape(-1)]).astype(F32)
    return jnp.where(jnp.asarray(par >= 0), flat[np.maximum(par, 0)], 0.0).reshape(1, LANES)


def _rotary_tables(seq):
    inv = ROPE_BASE ** (-jnp.arange(0, RET_DK, 2, dtype=F32) / RET_DK)
    ang = jnp.arange(seq, dtype=F32)[:, None] * inv[None, :]
    cos, sin = jnp.cos(ang), jnp.sin(ang)
    return jnp.concatenate([cos, cos], axis=1), jnp.concatenate([-sin, sin], axis=1)


def _trunk(x3, layers):
    bsz, seq, d = x3.shape
    x = x3.reshape(bsz * seq, d)
    cos_t, sin_t = _rotary_tables(seq)
    log_gamma = jnp.log1p(-jnp.exp2(-5.0 - jnp.arange(RET_HEADS, dtype=F32)))
    for p in layers:
        x = _ffn(x, *p["ffn1"], *p["ln1"])
        proj = _in_proj(x, p["w_in"])
        o_gdn = _gdn(proj, p["gdn_conv_w"], p["small_bias"], p["small_alog"], p["gdn_norm_w"], bsz, seq)
        o_ret = _ret(proj, log_gamma, cos_t, sin_t, p["ret_norm_g"], p["ret_norm_b"], bsz, seq)
        y_ssd = _ssd(proj, p["ssd_conv_w"], p["ssd_conv_b"], p["small_bias"], p["small_alog"], p["ssd_d"], bsz, seq)
        x = _out_proj(x, o_gdn, o_ret, y_ssd, *p["w_out"], p["ssd_norm_w"], *p["ln2"])
        x = _ffn(x, *p["ffn2"], *p["ln3"])
    return x.reshape(bsz, seq, d)


def kernel(x_prompt, x_sample, w_in, w_out, gdn_conv_w, gdn_a_log, gdn_dt_bias, gdn_norm_w, ret_norm_g, ret_norm_b,
           ssd_conv_w, ssd_conv_b, ssd_a_log, ssd_dt_bias, ssd_d, ssd_norm_w,
           ffn1_w_gate, ffn1_w_up, ffn1_w_down, ffn2_w_gate, ffn2_w_up, ffn2_w_down,
           ln1_g, ln1_b, ln2_g, ln2_b, ln3_g, ln3_b):
    row = lambda a: a.astype(F32).reshape(1, -1)
    layers = []
    for l in range(DEPTH):
        wo = w_out[l].astype(BF16)
        layers.append(dict(
            ffn1=(ffn1_w_gate[l].astype(BF16), ffn1_w_up[l].astype(BF16), ffn1_w_down[l].astype(BF16)),
            ffn2=(ffn2_w_gate[l].astype(BF16), ffn2_w_up[l].astype(BF16), ffn2_w_down[l].astype(BF16)),
            ln1=(row(ln1_g[l]), row(ln1_b[l])),
            ln2=(row(ln2_g[l]), row(ln2_b[l])),
            ln3=(row(ln3_g[l]), row(ln3_b[l])),
            w_in=_permute_w_in(w_in[l]),
            w_out=(wo[:GDN_W], wo[GDN_W:GDN_W + RET_W], wo[GDN_W + RET_W:]),
            gdn_conv_w=gdn_conv_w[l].astype(F32),
            small_bias=_small_lane_vector(gdn_dt_bias[l], ssd_dt_bias[l]),
            small_alog=_small_lane_vector(gdn_a_log[l], ssd_a_log[l]),
            gdn_norm_w=row(gdn_norm_w[l]),
            ret_norm_g=row(ret_norm_g[l]),
            ret_norm_b=row(ret_norm_b[l]),
            ssd_conv_w=ssd_conv_w[l].astype(F32),
            ssd_conv_b=row(ssd_conv_b[l]),
            ssd_d=jnp.repeat(ssd_d[l].astype(F32), SSD_P).reshape(1, SSD_W),
            ssd_norm_w=row(ssd_norm_w[l]),
        ))
    return _trunk(x_prompt, layers), _trunk(x_sample, layers)
```

```python
import functools
import math

import jax
import jax.numpy as jnp
import numpy as np
from jax import lax
from jax.experimental import pallas as pl
from jax.experimental.pallas import tpu as pltpu

F32 = jnp.float32
BF16 = jnp.bfloat16

D_MODEL = 2048
DEPTH = 2
CONV_K = 5
GDN_HEADS = 6
GDN_DK = 128
RET_HEADS = 4
RET_DK = 128
SSD_HEADS = 12
SSD_P = 64
SSD_N = 128
SSD_G = 2
D_FF = 5632
ROPE_BASE = 10000.0
LN_EPS = 1e-5
RMS_EPS = 1e-6
DEEPNORM_ALPHA = (2 * DEPTH) ** 0.25

GDN_W = GDN_HEADS * 128
RET_W = RET_HEADS * 128
SSD_W = SSD_HEADS * SSD_P
SSD_GROUP_W = SSD_W // SSD_G
SSD_PAIRS = SSD_HEADS // 2

LANES = 128
SUBLANES = 8
VMEM_LIMIT_BYTES = 60000 * 1024

CHUNK = 128
CONV_HALO = 8
LOCKSTEP_CHUNKS = 4
DEEP_LOCKSTEP_CHUNKS = 8

CB_SSD_X = 0
CB_SSD_Z = 6
CB_GDN_Q = 12
CB_GDN_K = 18
CB_GDN_V = 24
CB_GDN_Z = 30
CB_RET_Q = 36
CB_RET_K = 40
CB_RET_V = 44
CB_RET_G = 48
CB_SSD_B = 52
CB_SSD_C = 54
CB_SMALL = 56
N_COL_BLOCKS = 57
IN_PROJ_COL_GROUPS = 3
COL_BLOCKS_PER_GROUP = N_COL_BLOCKS // IN_PROJ_COL_GROUPS
SMALL_GDN_LANES = 4 * GDN_HEADS
SMALL_SSD_LANE0 = 32
SMALL_SSD_LANES = 4 * SSD_PAIRS

_OFF_GDN_QKV = 0
_OFF_GDN_Z = 2304
_OFF_GDN_B = 3072
_OFF_GDN_A = 3084
_OFF_RET = 3096
_OFF_SSD_XBC = 5144
_OFF_SSD_Z = 6424
_OFF_SSD_DT = 7192

FFN_TOKEN_TILE = 512
FFN_FF_TILE = 512
PROJ_TOKEN_TILE = 512


def _cparams(semantics):
    return pltpu.CompilerParams(dimension_semantics=semantics, vmem_limit_bytes=VMEM_LIMIT_BYTES)


def _mm(a, b):
    return jnp.dot(a.astype(BF16), b.astype(BF16), preferred_element_type=F32)


def _mm_nt(a, b):
    return lax.dot_general(a.astype(BF16), b.astype(BF16), (((1,), (1,)), ((), ())), preferred_element_type=F32)


def _mm_tn(a, b):
    return lax.dot_general(a.astype(BF16), b.astype(BF16), (((0,), (0,)), ((), ())), preferred_element_type=F32)


def _silu(x):
    return x * jax.nn.sigmoid(x)


def _softplus(x):
    return jnp.maximum(x, 0.0) + jnp.log1p(jnp.exp(-jnp.abs(x)))


def _layer_norm_rows(y, g, b):
    mu = jnp.mean(y, axis=-1, keepdims=True)
    yc = y - mu
    var = jnp.mean(yc * yc, axis=-1, keepdims=True)
    return yc * lax.rsqrt(var + LN_EPS) * g + b


def _ffn_kernel(x_ref, wg_ref, wu_ref, wd_ref, g_ref, b_ref, o_ref, xb_ref):
    j = pl.program_id(1)

    @pl.when(j == 0)
    def _():
        xb_ref[...] = x_ref[...].astype(BF16)
        o_ref[...] = jnp.zeros_like(o_ref)

    xb = xb_ref[...]
    gate = jnp.dot(xb, wg_ref[...], preferred_element_type=F32)
    up = jnp.dot(xb, wu_ref[...], preferred_element_type=F32)
    h = (_silu(gate) * up).astype(BF16)
    o_ref[...] += jnp.dot(h, wd_ref[...], preferred_element_type=F32)

    @pl.when(j == pl.num_programs(1) - 1)
    def _():
        y = DEEPNORM_ALPHA * x_ref[...] + 0.5 * o_ref[...]
        o_ref[...] = _layer_norm_rows(y, g_ref[...], b_ref[...])


def _ffn(x, wg, wu, wd, g, b):
    t, d = x.shape
    f = wg.shape[1]
    tm, tf = FFN_TOKEN_TILE, FFN_FF_TILE
    return pl.pallas_call(
        _ffn_kernel,
        grid=(t // tm, f // tf),
        in_specs=[
            pl.BlockSpec((tm, d), lambda i, j: (i, 0)),
            pl.BlockSpec((d, tf), lambda i, j: (0, j)),
            pl.BlockSpec((d, tf), lambda i, j: (0, j)),
            pl.BlockSpec((tf, d), lambda i, j: (j, 0)),
            pl.BlockSpec((1, d), lambda i, j: (0, 0)),
            pl.BlockSpec((1, d), lambda i, j: (0, 0)),
        ],
        out_specs=pl.BlockSpec((tm, d), lambda i, j: (i, 0)),
        out_shape=jax.ShapeDtypeStruct((t, d), F32),
        scratch_shapes=[pltpu.VMEM((tm, d), BF16)],
        compiler_params=_cparams(("parallel", "arbitrary")),
        name="ffn",
    )(x, wg, wu, wd, g, b)


def _in_proj_kernel(x_ref, w_ref, o_ref):
    r = jnp.dot(x_ref[...].astype(BF16), w_ref[...], preferred_element_type=F32)
    for c in range(COL_BLOCKS_PER_GROUP):
        o_ref[c] = r[:, c * LANES:(c + 1) * LANES]


def _in_proj(x, w):
    t, d = x.shape
    tm = PROJ_TOKEN_TILE
    tn = COL_BLOCKS_PER_GROUP * LANES
    return pl.pallas_call(
        _in_proj_kernel,
        grid=(IN_PROJ_COL_GROUPS, t // tm),
        in_specs=[
            pl.BlockSpec((tm, d), lambda j, i: (i, 0)),
            pl.BlockSpec((d, tn), lambda j, i: (0, j)),
        ],
        out_specs=pl.BlockSpec((COL_BLOCKS_PER_GROUP, tm, LANES), lambda j, i: (j, i, 0)),
        out_shape=jax.ShapeDtypeStruct((N_COL_BLOCKS, t, LANES), F32),
        compiler_params=_cparams(("arbitrary", "arbitrary")),
        name="in_proj",
    )(x, w)


def _lockstep(gens):
    results = [None] * len(gens)
    live = list(enumerate(gens))
    while live:
        still = []
        for idx, gen in live:
            try:
                next(gen)
                still.append((idx, gen))
            except StopIteration as stop:
                results[idx] = stop.value
        live = still
    return results


def _chunk_loop(n, chunk_gen, lockstep=LOCKSTEP_CHUNKS):
    width = math.gcd(n, lockstep)

    def trip(t, carry):
        _lockstep([chunk_gen(t * width + u) for u in range(width)])
        return carry

    lax.fori_loop(0, n // width, trip, 0)


def _rows(ci):
    return pl.ds(pl.multiple_of(ci * CHUNK, CHUNK), CHUNK)


def _iota2(shape, axis):
    return lax.broadcasted_iota(jnp.int32, shape, axis)


def _tri_masks():
    i = _iota2((CHUNK, CHUNK), 0)
    j = _iota2((CHUNK, CHUNK), 1)
    return ((j <= i, j < i), (j >= i, j > i))


def _cumsum_mat():
    j = _iota2((CHUNK, 3 * CHUNK), 0)
    i = _iota2((CHUNK, 3 * CHUNK), 1)
    prefix = jnp.logical_and(i < CHUNK, j <= i)
    suffix_or_total = jnp.logical_and(i >= CHUNK, jnp.logical_or(j + CHUNK >= i, i >= 2 * CHUNK))
    return jnp.where(jnp.logical_or(prefix, suffix_or_total), 1.0, 0.0).astype(BF16)


def _chunk_sums(rows_list, cmat):
    hi = [r.astype(BF16).astype(F32) for r in rows_list]
    mid = [(r - h).astype(BF16).astype(F32) for r, h in zip(rows_list, hi)]
    lo = [r - h - m for r, h, m in zip(rows_list, hi, mid)]
    sums = jnp.dot(jnp.concatenate(hi + mid + lo, axis=0).astype(BF16), cmat, preferred_element_type=F32)
    k = len(rows_list)
    out = []
    for u in range(k):
        s = sum(sums[(part * k + u) * SUBLANES:(part * k + u + 1) * SUBLANES] for part in range(3))
        out.append((s[:, :CHUNK], s[:, CHUNK:2 * CHUNK], s[:, 2 * CHUNK:]))
    return out


def _lane_vector_rows(vec, tmp_ref, row0):
    tmp_ref[...] = jnp.broadcast_to(vec, (CHUNK, LANES)).T
    return tmp_ref[pl.ds(row0, SUBLANES), :]


def _gate_rows_loop(n, sm_ref, tmp_ref, rrow_ref, rcol_ref, row0, row_values, assemble):
    width = math.gcd(n, DEEP_LOCKSTEP_CHUNKS)
    cmat = _cumsum_mat()

    def trip(t, carry):
        chunks = [t * width + u for u in range(width)]
        for u, ci in enumerate(chunks):
            tmp_ref[u] = sm_ref[_rows(ci), :].T
        rows = [row_values(tmp_ref[u, pl.ds(row0, SUBLANES), :]) for u in range(width)]
        sums = _chunk_sums(rows, cmat)
        r16s = [assemble(r, *s) for r, s in zip(rows, sums)]
        for ci, r16 in zip(chunks, r16s):
            rrow_ref[pl.ds(pl.multiple_of(ci * 16, 16), 16), :] = r16
        for ci, r16 in zip(chunks, r16s):
            rcol_ref[_rows(ci), :] = _pad_rows_transpose(r16)
        return carry

    lax.fori_loop(0, n // width, trip, 0)


def _conv_silu(x_ref, xp_ref, w_ref, bias, dst_ref, seq, l2_scale=None):
    n = seq // CHUNK
    zeros = jnp.zeros((CONV_HALO, LANES), F32)
    xp_ref[0:CONV_HALO, :] = zeros
    xp_ref[seq + CONV_HALO:seq + 2 * CONV_HALO, :] = zeros

    def copy(ci):
        r0 = pl.multiple_of(ci * CHUNK, CHUNK)
        xp_ref[pl.ds(r0 + CONV_HALO, CHUNK), :] = x_ref[pl.ds(r0, CHUNK), :]
        return
        yield

    _chunk_loop(n, copy)
    w = w_ref[...]

    def conv(ci):
        r0 = pl.multiple_of(ci * CHUNK, CHUNK)
        acc = jnp.zeros((CHUNK, LANES), F32)
        for k in range(CONV_K):
            acc = acc + xp_ref[pl.ds(r0 + (CONV_HALO - CONV_K // 2 + k), CHUNK), :] * w[k:k + 1, :]
        if bias is not None:
            acc = acc + bias
        t = _silu(acc)
        if l2_scale is not None:
            ss = jnp.sum(t * t, axis=-1, keepdims=True)
            yield
            t = t * (lax.rsqrt(ss + RMS_EPS) * l2_scale)
        dst_ref[pl.ds(r0, CHUNK), :] = t
        return
        yield

    _chunk_loop(n, conv)


def _pad_rows_transpose(r16):
    padded = jnp.concatenate([r16, jnp.zeros((CHUNK - r16.shape[0], LANES), F32)], axis=0)
    return padded.T


def _unit_tri_inverse(lmat):
    i = _iota2((CHUNK, CHUNK), 0)
    j = _iota2((CHUNK, CHUNK), 1)

    def same(shift):
        return lax.shift_right_logical(i, shift) == lax.shift_right_logical(j, shift)

    eye = jnp.where(i == j, 1.0, 0.0).astype(F32)
    l16 = jnp.where(same(4), lmat, 0.0)
    x = eye - l16
    p = _mm(l16, l16)
    yield
    for _ in range(2):
        x, p = x + _mm(x, p), _mm(p, p)
        yield
    x = x + _mm(x, p)
    yield
    for shift in (5, 6, 7):
        off = jnp.where(jnp.logical_and(same(shift), jnp.logical_not(same(shift - 1))), lmat, 0.0)
        ex = _mm(off, x)
        yield
        x = x - _mm(x, ex)
        yield
    return x


def _gdn_chunk_terms(gram_kk, gram_qk, q, k, v, rt, r16, d, masks):
    incl, strict = masks[d]
    beta = rt[:, d:d + 1]
    gc_col = rt[:, 2 + d:3 + d]
    gc_row = r16[2 + d:3 + d, :]
    tot_col = rt[:, 10 + d:11 + d]
    decay = jnp.where(incl, jnp.exp(jnp.where(incl, gc_col - gc_row, 0.0)), 0.0)
    lmat = jnp.where(strict, gram_kk * beta * decay, 0.0)
    tinv = yield from _unit_tri_inverse(lmat)
    eg = jnp.exp(gc_col)
    sol = _mm(tinv, jnp.concatenate([v * beta, k * (beta * eg)], axis=1)).astype(BF16)
    yield
    kt = _mm_tn(k * jnp.exp(tot_col - gc_col), sol)
    qs = _mm(gram_qk * decay, sol)
    yield
    return kt[:, LANES:], kt[:, :LANES], q * eg - qs[:, LANES:], qs[:, :LANES]


def _gdn_kernel(q_ref, k_ref, v_ref, z_ref, sm_ref, cwq_ref, cwk_ref, cwv_ref, bias_ref, alog_ref, nw_ref,
                o_ref, xp_ref, qn_ref, kn_ref, vn_ref, rrow_ref, rcol_ref, of_ref, ob_ref, mq_ref, nn_ref,
                tmp_ref, *, seq):
    n = seq // CHUNK
    h = pl.program_id(1)

    _conv_silu(q_ref, xp_ref, cwq_ref, None, qn_ref, seq, l2_scale=GDN_DK ** -0.5)
    _conv_silu(k_ref, xp_ref, cwk_ref, None, kn_ref, seq, l2_scale=1.0)
    _conv_silu(v_ref, xp_ref, cwv_ref, None, vn_ref, seq)

    row0 = 4 * h
    row = _iota2((SUBLANES, LANES), 0)
    bias8 = _lane_vector_rows(bias_ref[...], tmp_ref.at[0], row0)
    neg_a8 = -jnp.exp(_lane_vector_rows(alog_ref[...], tmp_ref.at[0], row0))

    def gate_values(raw8):
        return jnp.where(row < 2, jax.nn.sigmoid(raw8), neg_a8 * _softplus(raw8 + bias8))

    def gate_rows(vals, prefix, suffix, total):
        return jnp.concatenate([jnp.where(row < 2, vals, jnp.where(row == 2, prefix, suffix)), total], axis=0)

    _gate_rows_loop(n, sm_ref, tmp_ref, rrow_ref, rcol_ref, row0, gate_values, gate_rows)

    masks = _tri_masks()
    outs = (of_ref, ob_ref)

    def local_terms(ci):
        q = qn_ref[_rows(ci), :]
        k = kn_ref[_rows(ci), :]
        v = vn_ref[_rows(ci), :]
        rt = rcol_ref[_rows(ci), :]
        r16 = rrow_ref[pl.ds(pl.multiple_of(ci * 16, 16), 16), :]
        grams = _mm_nt(jnp.concatenate([k, q], axis=0), k)
        yield
        terms = yield from _lockstep_gen(
            [_gdn_chunk_terms(grams[:CHUNK], grams[CHUNK:], q, k, v, rt, r16, d, masks) for d in range(2)])
        for d, (m_, n_, q_, o_) in enumerate(terms):
            mq_ref[d, pl.ds(pl.multiple_of(2 * ci * CHUNK, CHUNK), 2 * CHUNK), :] = (
                jnp.concatenate([m_, q_], axis=0).astype(BF16))
            nn_ref[d, _rows(ci), :] = n_
            outs[d][_rows(ci), :] = o_

    _chunk_loop(n, local_terms, lockstep=DEEP_LOCKSTEP_CHUNKS)

    def advance(d, ci, state):
        m = jnp.dot(mq_ref[d, pl.ds(pl.multiple_of(2 * ci * CHUNK, CHUNK), 2 * CHUNK), :], state.astype(BF16),
                    preferred_element_type=F32)
        yield
        outs[d][_rows(ci), :] += m[CHUNK:]
        tot_row = rrow_ref[pl.ds(ci * 16 + 10 + d, 1), :]
        return state * jnp.exp(jnp.broadcast_to(tot_row, (GDN_DK, LANES))) - m[:CHUNK] + nn_ref[d, _rows(ci), :]

    def sweep(i, carry):
        return tuple(_lockstep([advance(0, i, carry[0]), advance(1, n - 1 - i, carry[1])]))

    zero_state = jnp.zeros((GDN_DK, LANES), F32)
    lax.fori_loop(0, n, sweep, (zero_state, zero_state))

    nw = nw_ref[...]

    def finish(ci):
        o = of_ref[_rows(ci), :] + ob_ref[_rows(ci), :]
        ms = jnp.mean(o * o, axis=-1, keepdims=True)
        yield
        o = o * (lax.rsqrt(ms + RMS_EPS) * nw)
        o_ref[_rows(ci), :] = (o * _silu(z_ref[_rows(ci), :])).astype(o_ref.dtype)

    _chunk_loop(n, finish)


def _lockstep_gen(gens):
    results = [None] * len(gens)
    live = list(enumerate(gens))
    while live:
        still = []
        for idx, gen in live:
            try:
                next(gen)
                still.append((idx, gen))
            except StopIteration as stop:
                results[idx] = stop.value
        live = still
        if live:
            yield
    return results


def _proj_block(cb0):
    return lambda seq: pl.BlockSpec((None, seq, LANES), lambda b, h: (cb0 + h, b, 0))


def _gdn(proj, conv_w, bias_vec, alog_vec, norm_w, bsz, seq):
    n = seq // CHUNK
    col = lambda c0: pl.BlockSpec((CONV_K, LANES), lambda b, h: (0, c0 + h))
    vec = pl.BlockSpec((1, LANES), lambda b, h: (0, 0))
    seq_f32 = pltpu.VMEM((seq, LANES), F32)
    return pl.pallas_call(
        functools.partial(_gdn_kernel, seq=seq),
        grid=(bsz, GDN_HEADS),
        in_specs=[
            _proj_block(CB_GDN_Q)(seq), _proj_block(CB_GDN_K)(seq), _proj_block(CB_GDN_V)(seq),
            _proj_block(CB_GDN_Z)(seq),
            pl.BlockSpec((None, seq, LANES), lambda b, h: (CB_SMALL, b, 0)),
            col(0), col(GDN_HEADS), col(2 * GDN_HEADS), vec, vec, vec,
        ],
        out_specs=pl.BlockSpec((seq, LANES), lambda b, h: (b, h)),
        out_shape=jax.ShapeDtypeStruct((bsz * seq, GDN_W), BF16),
        scratch_shapes=[
            pltpu.VMEM((seq + 2 * CONV_HALO, LANES), F32),
            seq_f32, seq_f32, seq_f32,
            pltpu.VMEM((n * 16, LANES), F32),
            seq_f32, seq_f32, seq_f32,
            pltpu.VMEM((2, 2 * seq, LANES), BF16),
            pltpu.VMEM((2, seq, LANES), F32),
            pltpu.VMEM((DEEP_LOCKSTEP_CHUNKS, CHUNK, LANES), F32),
        ],
        compiler_params=_cparams(("parallel", "arbitrary")),
        name="gdn",
    )(proj, proj, proj, proj, proj, conv_w, conv_w, conv_w, bias_vec, alog_vec, norm_w)


def _ret_kernel(lg_ref, q_ref, k_ref, v_ref, g_ref, cos_ref, sin_ref, ng_ref, nb_ref,
                o_ref, of_ref, ob_ref, *, seq):
    n = seq // CHUNK
    width = math.gcd(n, LOCKSTEP_CHUNKS)
    lg = lg_ref[pl.program_id(1)]
    i = _iota2((CHUNK, CHUNK), 0).astype(F32)
    j = _iota2((CHUNK, CHUNK), 1).astype(F32)
    d_sym = jnp.exp(jnp.abs(i - j) * lg)
    q_dec = (jnp.exp((i + 1.0) * lg), jnp.exp((CHUNK - i) * lg))
    k_dec = (jnp.exp((CHUNK - 1.0 - i) * lg), jnp.exp(i * lg))
    chunk_decay = jnp.exp(jnp.full((RET_DK, LANES), CHUNK, F32) * lg)
    outs = (of_ref, ob_ref)

    def rot(t, ci):
        return t * cos_ref[_rows(ci), :] + pltpu.roll(t, LANES // 2, axis=1) * sin_ref[_rows(ci), :]

    def chunk_terms(d, ci):
        q = rot(q_ref[_rows(ci), :], ci)
        k = rot(k_ref[_rows(ci), :], ci) * (RET_DK ** -0.5)
        v = v_ref[_rows(ci), :]
        kv = _mm_tn(k * k_dec[d], v)
        scores = _mm_nt(q, k) * d_sym if d == 0 else None
        yield
        intra = _mm(scores, v) if d == 0 else None
        return q, kv, intra

    def sweep(t, carry):
        chunks = ([t * width + u for u in range(width)], [n - 1 - (t * width + u) for u in range(width)])
        terms = _lockstep([chunk_terms(d, ci) for d in range(2) for ci in chunks[d]])
        states = list(carry)
        for d in range(2):
            for u, ci in enumerate(chunks[d]):
                q, kv, intra = terms[d * width + u]
                inter = _mm(q * q_dec[d], states[d])
                outs[d][_rows(ci), :] = inter if intra is None else intra + inter
                states[d] = states[d] * chunk_decay + kv
        return tuple(states)

    zero_state = jnp.zeros((RET_DK, LANES), F32)
    lax.fori_loop(0, n // width, sweep, (zero_state, zero_state))

    ng = ng_ref[...]
    nb = nb_ref[...]

    def finish(ci):
        o = of_ref[_rows(ci), :] + ob_ref[_rows(ci), :]
        mu = jnp.mean(o, axis=-1, keepdims=True)
        yield
        oc = o - mu
        var = jnp.mean(oc * oc, axis=-1, keepdims=True)
        yield
        o = oc * (lax.rsqrt(var + LN_EPS) * ng) + nb
        o_ref[_rows(ci), :] = (_silu(g_ref[_rows(ci), :]) * o).astype(o_ref.dtype)

    _chunk_loop(n, finish)


def _ret(proj, log_gamma, cos_t, sin_t, norm_g, norm_b, bsz, seq):
    table = pl.BlockSpec((seq, LANES), lambda b, h: (0, 0))
    vec = pl.BlockSpec((1, LANES), lambda b, h: (0, h))
    seq_f32 = pltpu.VMEM((seq, LANES), F32)
    return pl.pallas_call(
        functools.partial(_ret_kernel, seq=seq),
        grid=(bsz, RET_HEADS),
        in_specs=[
            pl.BlockSpec(memory_space=pltpu.SMEM),
            _proj_block(CB_RET_Q)(seq), _proj_block(CB_RET_K)(seq), _proj_block(CB_RET_V)(seq),
            _proj_block(CB_RET_G)(seq),
            table, table, vec, vec,
        ],
        out_specs=pl.BlockSpec((seq, LANES), lambda b, h: (b, h)),
        out_shape=jax.ShapeDtypeStruct((bsz * seq, RET_W), BF16),
        scratch_shapes=[seq_f32, seq_f32],
        compiler_params=_cparams(("parallel", "arbitrary")),
        name="ret",
    )(log_gamma, proj, proj, proj, proj, cos_t, sin_t, norm_g, norm_b)


def _ssd_kernel(x_ref, z_ref, b_ref, c_ref, sm_ref, cwx_ref, cwb_ref, cwc_ref, cbx_ref, cbb_ref, cbc_ref,
                bias_ref, alog_ref, dvec_ref, o_ref, xp_ref, xs_ref, bm_ref, cm_ref, rrow_ref, rcol_ref,
                of_ref, ob_ref, st_ref, sc_ref, tmp_ref, *, seq):
    n = seq // CHUNK
    width = math.gcd(n, LOCKSTEP_CHUNKS)
    p = pl.program_id(1)
    _conv_silu(x_ref, xp_ref, cwx_ref, cbx_ref[...], xs_ref, seq)
    _conv_silu(b_ref, xp_ref, cwb_ref, cbb_ref[...], bm_ref, seq)
    _conv_silu(c_ref, xp_ref, cwc_ref, cbc_ref[...], cm_ref, seq)

    row0 = SMALL_SSD_LANE0 + 4 * p
    row = _iota2((SUBLANES, LANES), 0)

    def rows_twice(x8):
        return jnp.where(row < 4, x8, pltpu.roll(x8, 4, axis=0))

    bias8 = _lane_vector_rows(bias_ref[...], tmp_ref.at[0], row0)
    neg_a8 = rows_twice(-jnp.exp(_lane_vector_rows(alog_ref[...], tmp_ref.at[0], row0)))

    def gate_values(raw8):
        dt = rows_twice(_softplus(raw8 + bias8))
        return jnp.where(row < 4, dt, dt * neg_a8)

    def gate_rows(vals, prefix, suffix, total):
        return jnp.concatenate([jnp.where(row < 4, vals, jnp.where(row < 6, prefix, suffix)), total], axis=0)

    _gate_rows_loop(n, sm_ref, tmp_ref, rrow_ref, rcol_ref, row0, gate_values, gate_rows)

    masks = _tri_masks()
    first = _iota2((CHUNK, LANES), 1) < SSD_P
    outs = (of_ref, ob_ref)

    def total_rows(ci, d):
        return [jnp.broadcast_to(rrow_ref[pl.ds(ci * 16 + 12 + 2 * d + e, 1), :], (CHUNK, LANES)) for e in range(2)]

    def local_terms(ci):
        x = xs_ref[_rows(ci), :]
        bm = bm_ref[_rows(ci), :]
        cm = cm_ref[_rows(ci), :]
        rt = rcol_ref[_rows(ci), :]
        r16 = rrow_ref[pl.ds(pl.multiple_of(ci * 16, 16), 16), :]
        cb = _mm_nt(cm, bm)
        bm_t = bm.T
        x_heads = jnp.concatenate([jnp.where(first, x, 0.0), jnp.where(first, 0.0, x)], axis=0).astype(BF16)
        yield
        for d in range(2):
            incl = masks[d][0]
            tot = total_rows(ci, d)
            lc_cols, m_parts, g_parts = [], [], []
            for e in range(2):
                dt_row = r16[2 * d + e:2 * d + e + 1, :]
                lc_row = r16[4 + 2 * d + e:5 + 2 * d + e, :]
                lc_col = jnp.broadcast_to(rt[:, 4 + 2 * d + e:5 + 2 * d + e], (CHUNK, LANES))
                seg = jnp.where(incl, jnp.exp(jnp.where(incl, lc_col - lc_row, 0.0)), 0.0)
                m_parts.append(cb * seg * dt_row)
                g_parts.append(bm_t * (dt_row * jnp.exp(tot[e][0:1, :] - lc_row)))
                lc_cols.append(lc_col)
            lhs = jnp.concatenate([jnp.concatenate(m_parts, axis=1), jnp.concatenate(g_parts, axis=1)], axis=0)
            both = jnp.dot(lhs.astype(BF16), x_heads, preferred_element_type=F32)
            outs[d][_rows(ci), :] = both[:CHUNK]
            st_ref[d, _rows(ci), :] = both[CHUNK:]
            sc_ref[d, _rows(ci), :] = jnp.exp(jnp.where(first, lc_cols[0], lc_cols[1]))

    _chunk_loop(n, local_terms)

    def sweep(t, carry):
        chunks = ([t * width + u for u in range(width)], [n - 1 - (t * width + u) for u in range(width)])
        states = list(carry)
        for d in range(2):
            for ci in chunks[d]:
                outs[d][_rows(ci), :] += _mm(cm_ref[_rows(ci), :], states[d]) * sc_ref[d, _rows(ci), :]
                tot = total_rows(ci, d)
                states[d] = states[d] * jnp.exp(jnp.where(first, tot[0], tot[1])) + st_ref[d, _rows(ci), :]
        return tuple(states)

    zero_state = jnp.zeros((SSD_N, 2 * SSD_P), F32)
    lax.fori_loop(0, n // width, sweep, (zero_state, zero_state))

    dvec = dvec_ref[...]

    def finish(ci):
        y = of_ref[_rows(ci), :] + ob_ref[_rows(ci), :] + dvec * xs_ref[_rows(ci), :]
        o_ref[_rows(ci), :] = y * _silu(z_ref[_rows(ci), :])
        return
        yield

    _chunk_loop(n, finish)


def _ssd(proj, conv_w, conv_b, bias_vec, alog_vec, dvec, bsz, seq):
    n = seq // CHUNK
    group = lambda p: p // (SSD_PAIRS // SSD_G)
    wcol = lambda fn: pl.BlockSpec((CONV_K, LANES), lambda b, p: (0, fn(p)))
    bcol = lambda fn: pl.BlockSpec((1, LANES), lambda b, p: (0, fn(p)))
    vec = pl.BlockSpec((1, LANES), lambda b, p: (0, 0))
    seq_f32 = pltpu.VMEM((seq, LANES), F32)
    x_col = lambda p: p
    b_col = lambda p: SSD_W // LANES + group(p)
    c_col = lambda p: SSD_W // LANES + SSD_G + group(p)
    return pl.pallas_call(
        functools.partial(_ssd_kernel, seq=seq),
        grid=(bsz, SSD_PAIRS),
        in_specs=[
            _proj_block(CB_SSD_X)(seq), _proj_block(CB_SSD_Z)(seq),
            pl.BlockSpec((None, seq, LANES), lambda b, p: (CB_SSD_B + group(p), b, 0)),
            pl.BlockSpec((None, seq, LANES), lambda b, p: (CB_SSD_C + group(p), b, 0)),
            pl.BlockSpec((None, seq, LANES), lambda b, p: (CB_SMALL, b, 0)),
            wcol(x_col), wcol(b_col), wcol(c_col), bcol(x_col), bcol(b_col), bcol(c_col),
            vec, vec, bcol(x_col),
        ],
        out_specs=pl.BlockSpec((seq, LANES), lambda b, p: (b, p)),
        out_shape=jax.ShapeDtypeStruct((bsz * seq, SSD_W), F32),
        scratch_shapes=[
            pltpu.VMEM((seq + 2 * CONV_HALO, LANES), F32),
            seq_f32, seq_f32, seq_f32,
            pltpu.VMEM((n * 16, LANES), F32),
            seq_f32, seq_f32, seq_f32,
            pltpu.VMEM((2, seq, LANES), F32),
            pltpu.VMEM((2, seq, LANES), F32),
            pltpu.VMEM((DEEP_LOCKSTEP_CHUNKS, CHUNK, LANES), F32),
        ],
        compiler_params=_cparams(("parallel", "arbitrary")),
        name="ssd",
    )(proj, proj, proj, proj, proj, conv_w, conv_w, conv_w, conv_b, conv_b, conv_b, bias_vec, alog_vec, dvec)


def _out_proj_kernel(x_ref, og_ref, or_ref, y_ref, wg_ref, wr_ref, wy_ref, nw_ref, g_ref, b_ref, o_ref):
    acc = jnp.dot(og_ref[...], wg_ref[...], preferred_element_type=F32)
    acc = acc + jnp.dot(or_ref[...], wr_ref[...], preferred_element_type=F32)
    y = y_ref[...]
    nw = nw_ref[...]
    for g in range(SSD_G):
        lo, hi = g * SSD_GROUP_W, (g + 1) * SSD_GROUP_W
        yg = y[:, lo:hi]
        yg = yg * lax.rsqrt(jnp.mean(yg * yg, axis=-1, keepdims=True) + RMS_EPS) * nw[:, lo:hi]
        acc = acc + jnp.dot(yg.astype(BF16), wy_ref[lo:hi, :], preferred_element_type=F32)
    o_ref[...] = _layer_norm_rows(DEEPNORM_ALPHA * x_ref[...] + acc, g_ref[...], b_ref[...])


def _out_proj(x, o_gdn, o_ret, y_ssd, w_gdn, w_ret, w_ssd, ssd_norm_w, g, b):
    t, d = x.shape
    tm = PROJ_TOKEN_TILE
    rows = lambda width: pl.BlockSpec((tm, width), lambda i: (i, 0))
    whole = lambda a: pl.BlockSpec(a.shape, lambda i: (0, 0))
    return pl.pallas_call(
        _out_proj_kernel,
        grid=(t // tm,),
        in_specs=[
            rows(d), rows(GDN_W), rows(RET_W), rows(SSD_W),
            whole(w_gdn), whole(w_ret), whole(w_ssd), whole(ssd_norm_w), whole(g), whole(b),
        ],
        out_specs=rows(d),
        out_shape=jax.ShapeDtypeStruct((t, d), F32),
        compiler_params=_cparams(("parallel",)),
        name="out_proj",
    )(x, o_gdn, o_ret, y_ssd, w_gdn, w_ret, w_ssd, ssd_norm_w, g, b)


def _small_lane_sources():
    col = -np.ones((LANES,), np.int64)
    par = -np.ones((LANES,), np.int64)
    for h in range(GDN_HEADS):
        for d in range(2):
            col[4 * h + d] = _OFF_GDN_B + d * GDN_HEADS + h
            col[4 * h + 2 + d] = _OFF_GDN_A + d * GDN_HEADS + h
            par[4 * h + 2 + d] = d * GDN_HEADS + h
    for p in range(SSD_PAIRS):
        for d in range(2):
            for e in range(2):
                lane = SMALL_SSD_LANE0 + 4 * p + 2 * d + e
                col[lane] = _OFF_SSD_DT + d * SSD_HEADS + 2 * p + e
                par[lane] = 2 * GDN_HEADS + d * SSD_HEADS + 2 * p + e
    return col, par


def _permute_w_in(w):
    col, _ = _small_lane_sources()
    small = jnp.where(jnp.asarray(col >= 0)[None, :], w[:, np.maximum(col, 0)], 0.0)
    xbc = _OFF_SSD_XBC
    return jnp.concatenate([
        w[:, xbc:xbc + SSD_W],
        w[:, _OFF_SSD_Z:_OFF_SSD_Z + SSD_W],
        w[:, _OFF_GDN_QKV:_OFF_GDN_B],
        w[:, _OFF_RET:_OFF_SSD_XBC],
        w[:, xbc + SSD_W:_OFF_SSD_Z],
        small,
    ], axis=1).astype(BF16)


def _small_lane_vector(gdn_param, ssd_param):
    _, par = _small_lane_sources()
    flat = jnp.concatenate([gdn_param.reshape(-1), ssd_param.reshape(-1)]).astype(F32)
    return jnp.where(jnp.asarray(par >= 0), flat[np.maximum(par, 0)], 0.0).reshape(1, LANES)


def _rotary_tables(seq):
    inv = ROPE_BASE ** (-jnp.arange(0, RET_DK, 2, dtype=F32) / RET_DK)
    ang = jnp.arange(seq, dtype=F32)[:, None] * inv[None, :]
    cos, sin = jnp.cos(ang), jnp.sin(ang)
    return jnp.concatenate([cos, cos], axis=1), jnp.concatenate([-sin, sin], axis=1)


def _trunk(x3, layers):
    bsz, seq, d = x3.shape
    x = x3.reshape(bsz * seq, d)
    cos_t, sin_t = _rotary_tables(seq)
    log_gamma = jnp.log1p(-jnp.exp2(-5.0 - jnp.arange(RET_HEADS, dtype=F32)))
    for p in layers:
        x = _ffn(x, *p["ffn1"], *p["ln1"])
        proj = _in_proj(x, p["w_in"])
        o_gdn = _gdn(proj, p["gdn_conv_w"], p["small_bias"], p["small_alog"], p["gdn_norm_w"], bsz, seq)
        o_ret = _ret(proj, log_gamma, cos_t, sin_t, p["ret_norm_g"], p["ret_norm_b"], bsz, seq)
        y_ssd = _ssd(proj, p["ssd_conv_w"], p["ssd_conv_b"], p["small_bias"], p["small_alog"], p["ssd_d"], bsz, seq)
        x = _out_proj(x, o_gdn, o_ret, y_ssd, *p["w_out"], p["ssd_norm_w"], *p["ln2"])
        x = _ffn(x, *p["ffn2"], *p["ln3"])
    return x.reshape(bsz, seq, d)


def kernel(x_prompt, x_sample, w_in, w_out, gdn_conv_w, gdn_a_log, gdn_dt_bias, gdn_norm_w, ret_norm_g, ret_norm_b,
           ssd_conv_w, ssd_conv_b, ssd_a_log, ssd_dt_bias, ssd_d, ssd_norm_w,
           ffn1_w_gate, ffn1_w_up, ffn1_w_down, ffn2_w_gate, ffn2_w_up, ffn2_w_down,
           ln1_g, ln1_b, ln2_g, ln2_b, ln3_g, ln3_b):
    row = lambda a: a.astype(F32).reshape(1, -1)
    layers = []
    for l in range(DEPTH):
        wo = w_out[l].astype(BF16)
        layers.append(dict(
            ffn1=(ffn1_w_gate[l].astype(BF16), ffn1_w_up[l].astype(BF16), ffn1_w_down[l].astype(BF16)),
            ffn2=(ffn2_w_gate[l].astype(BF16), ffn2_w_up[l].astype(BF16), ffn2_w_down[l].astype(BF16)),
            ln1=(row(ln1_g[l]), row(ln1_b[l])),
            ln2=(row(ln2_g[l]), row(ln2_b[l])),
            ln3=(row(ln3_g[l]), row(ln3_b[l])),
            w_in=_permute_w_in(w_in[l]),
            w_out=(wo[:GDN_W], wo[GDN_W:GDN_W + RET_W], wo[GDN_W + RET_W:]),
            gdn_conv_w=gdn_conv_w[l].astype(F32),
            small_bias=_small_lane_vector(gdn_dt_bias[l], ssd_dt_bias[l]),
            small_alog=_small_lane_vector(gdn_a_log[l], ssd_a_log[l]),
            gdn_norm_w=row(gdn_norm_w[l]),
            ret_norm_g=row(ret_norm_g[l]),
            ret_norm_b=row(ret_norm_b[l]),
            ssd_conv_w=ssd_conv_w[l].astype(F32),
            ssd_conv_b=row(ssd_conv_b[l]),
            ssd_d=jnp.repeat(ssd_d[l].astype(F32), SSD_P).reshape(1, SSD_W),
            ssd_norm_w=row(ssd_norm_w[l]),
        ))
    return _trunk(x_prompt, layers), _trunk(x_sample, layers)
```

```python
import functools
import math

import jax
import jax.numpy as jnp
import numpy as np
from jax import lax
from jax.experimental import pallas as pl
from jax.experimental.pallas import tpu as pltpu

F32 = jnp.float32
BF16 = jnp.bfloat16

D_MODEL = 2048
DEPTH = 2
CONV_K = 5
GDN_HEADS = 6
GDN_DK = 128
RET_HEADS = 4
RET_DK = 128
SSD_HEADS = 12
SSD_P = 64
SSD_N = 128
SSD_G = 2
D_FF = 5632
ROPE_BASE = 10000.0
LN_EPS = 1e-5
RMS_EPS = 1e-6
DEEPNORM_ALPHA = (2 * DEPTH) ** 0.25

GDN_W = GDN_HEADS * 128
RET_W = RET_HEADS * 128
SSD_W = SSD_HEADS * SSD_P
SSD_GROUP_W = SSD_W // SSD_G
SSD_PAIRS = SSD_HEADS // 2

LANES = 128
SUBLANES = 8
VMEM_LIMIT_BYTES = 60000 * 1024

CHUNK = 128
CONV_HALO = 8
LOCKSTEP_CHUNKS = 4
DEEP_LOCKSTEP_CHUNKS = 8

CB_SSD_X = 0
CB_SSD_Z = 6
CB_GDN_Q = 12
CB_GDN_K = 18
CB_GDN_V = 24
CB_GDN_Z = 30
CB_RET_Q = 36
CB_RET_K = 40
CB_RET_V = 44
CB_RET_G = 48
CB_SSD_B = 52
CB_SSD_C = 54
CB_SMALL = 56
N_COL_BLOCKS = 57
IN_PROJ_COL_GROUPS = 3
COL_BLOCKS_PER_GROUP = N_COL_BLOCKS // IN_PROJ_COL_GROUPS
SMALL_GDN_LANES = 4 * GDN_HEADS
SMALL_SSD_LANE0 = 32
SMALL_SSD_LANES = 4 * SSD_PAIRS

_OFF_GDN_QKV = 0
_OFF_GDN_Z = 2304
_OFF_GDN_B = 3072
_OFF_GDN_A = 3084
_OFF_RET = 3096
_OFF_SSD_XBC = 5144
_OFF_SSD_Z = 6424
_OFF_SSD_DT = 7192

FFN_TOKEN_TILE = 1024
FFN_FF_TILE = 512
PROJ_TOKEN_TILE = 512


def _cparams(semantics):
    return pltpu.CompilerParams(dimension_semantics=semantics, vmem_limit_bytes=VMEM_LIMIT_BYTES)


def _mm(a, b):
    return jnp.dot(a.astype(BF16), b.astype(BF16), preferred_element_type=F32)


def _mm_nt(a, b):
    return lax.dot_general(a.astype(BF16), b.astype(BF16), (((1,), (1,)), ((), ())), preferred_element_type=F32)


def _mm_tn(a, b):
    return lax.dot_general(a.astype(BF16), b.astype(BF16), (((0,), (0,)), ((), ())), preferred_element_type=F32)


def _silu(x):
    return x * jax.nn.sigmoid(x)


def _softplus(x):
    return jnp.maximum(x, 0.0) + jnp.log1p(jnp.exp(-jnp.abs(x)))


def _layer_norm_rows(y, g, b):
    mu = jnp.mean(y, axis=-1, keepdims=True)
    yc = y - mu
    var = jnp.mean(yc * yc, axis=-1, keepdims=True)
    return yc * lax.rsqrt(var + LN_EPS) * g + b


def _ffn_kernel(x_ref, wg_ref, wu_ref, wd_ref, g_ref, b_ref, o_ref):
    j = pl.program_id(1)

    @pl.when(j == 0)
    def _():
        o_ref[...] = jnp.zeros_like(o_ref)

    xb = x_ref[...].astype(BF16)
    gate = jnp.dot(xb, wg_ref[...], preferred_element_type=F32)
    up = jnp.dot(xb, wu_ref[...], preferred_element_type=F32)
    h = (_silu(gate) * up).astype(BF16)
    o_ref[...] += jnp.dot(h, wd_ref[...], preferred_element_type=F32)

    @pl.when(j == pl.num_programs(1) - 1)
    def _():
        y = DEEPNORM_ALPHA * x_ref[...] + 0.5 * o_ref[...]
        o_ref[...] = _layer_norm_rows(y, g_ref[...], b_ref[...])


def _ffn(x, wg, wu, wd, g, b):
    t, d = x.shape
    f = wg.shape[1]
    tm, tf = FFN_TOKEN_TILE, FFN_FF_TILE
    return pl.pallas_call(
        _ffn_kernel,
        grid=(t // tm, f // tf),
        in_specs=[
            pl.BlockSpec((tm, d), lambda i, j: (i, 0)),
            pl.BlockSpec((d, tf), lambda i, j: (0, j)),
            pl.BlockSpec((d, tf), lambda i, j: (0, j)),
            pl.BlockSpec((tf, d), lambda i, j: (j, 0)),
            pl.BlockSpec((1, d), lambda i, j: (0, 0)),
            pl.BlockSpec((1, d), lambda i, j: (0, 0)),
        ],
        out_specs=pl.BlockSpec((tm, d), lambda i, j: (i, 0)),
        out_shape=jax.ShapeDtypeStruct((t, d), F32),
        compiler_params=_cparams(("parallel", "arbitrary")),
        name="ffn",
    )(x, wg, wu, wd, g, b)


def _in_proj_kernel(x_ref, w_ref, o_ref):
    r = jnp.dot(x_ref[...].astype(BF16), w_ref[...], preferred_element_type=F32)
    for c in range(COL_BLOCKS_PER_GROUP):
        o_ref[c] = r[:, c * LANES:(c + 1) * LANES]


def _in_proj(x, w):
    t, d = x.shape
    tm = PROJ_TOKEN_TILE
    tn = COL_BLOCKS_PER_GROUP * LANES
    return pl.pallas_call(
        _in_proj_kernel,
        grid=(IN_PROJ_COL_GROUPS, t // tm),
        in_specs=[
            pl.BlockSpec((tm, d), lambda j, i: (i, 0)),
            pl.BlockSpec((d, tn), lambda j, i: (0, j)),
        ],
        out_specs=pl.BlockSpec((COL_BLOCKS_PER_GROUP, tm, LANES), lambda j, i: (j, i, 0)),
        out_shape=jax.ShapeDtypeStruct((N_COL_BLOCKS, t, LANES), F32),
        compiler_params=_cparams(("arbitrary", "arbitrary")),
        name="in_proj",
    )(x, w)


def _lockstep(gens):
    results = [None] * len(gens)
    live = list(enumerate(gens))
    while live:
        still = []
        for idx, gen in live:
            try:
                next(gen)
                still.append((idx, gen))
            except StopIteration as stop:
                results[idx] = stop.value
        live = still
    return results


def _chunk_loop(n, chunk_gen):
    width = math.gcd(n, LOCKSTEP_CHUNKS)

    def trip(t, carry):
        _lockstep([chunk_gen(t * width + u) for u in range(width)])
        return carry

    lax.fori_loop(0, n // width, trip, 0)


def _rows(ci):
    return pl.ds(pl.multiple_of(ci * CHUNK, CHUNK), CHUNK)


def _iota2(shape, axis):
    return lax.broadcasted_iota(jnp.int32, shape, axis)


def _tri_masks():
    i = _iota2((CHUNK, CHUNK), 0)
    j = _iota2((CHUNK, CHUNK), 1)
    return ((j <= i, j < i), (j >= i, j > i))


def _cumsum_mat():
    j = _iota2((CHUNK, 3 * CHUNK), 0)
    i = _iota2((CHUNK, 3 * CHUNK), 1)
    prefix = jnp.logical_and(i < CHUNK, j <= i)
    suffix_or_total = jnp.logical_and(i >= CHUNK, jnp.logical_or(j + CHUNK >= i, i >= 2 * CHUNK))
    return jnp.where(jnp.logical_or(prefix, suffix_or_total), 1.0, 0.0).astype(BF16)


def _chunk_sums(rows_list, cmat):
    hi = [r.astype(BF16).astype(F32) for r in rows_list]
    mid = [(r - h).astype(BF16).astype(F32) for r, h in zip(rows_list, hi)]
    lo = [r - h - m for r, h, m in zip(rows_list, hi, mid)]
    sums = jnp.dot(jnp.concatenate(hi + mid + lo, axis=0).astype(BF16), cmat, preferred_element_type=F32)
    k = len(rows_list)
    out = []
    for u in range(k):
        s = sum(sums[(part * k + u) * SUBLANES:(part * k + u + 1) * SUBLANES] for part in range(3))
        out.append((s[:, :CHUNK], s[:, CHUNK:2 * CHUNK], s[:, 2 * CHUNK:]))
    return out


def _lane_vector_rows(vec, tmp_ref, row0):
    tmp_ref[...] = jnp.broadcast_to(vec, (CHUNK, LANES)).T
    return tmp_ref[pl.ds(row0, SUBLANES), :]


def _gate_rows(sm_ref, tmp_ref, rrow_ref, rcol_ref, row0, row_values, assemble):
    cmat = _cumsum_mat()

    def trip(chunks):
        for u, ci in enumerate(chunks):
            tmp_ref[u] = sm_ref[_rows(ci), :].T
        yield
        rows = [row_values(tmp_ref[u, pl.ds(row0, SUBLANES), :]) for u in range(len(chunks))]
        yield
        sums = _chunk_sums(rows, cmat)
        yield
        r16s = [assemble(r, *s) for r, s in zip(rows, sums)]
        for ci, r16 in zip(chunks, r16s):
            rrow_ref[pl.ds(pl.multiple_of(ci * 16, 16), 16), :] = r16
        for ci, r16 in zip(chunks, r16s):
            rcol_ref[_rows(ci), :] = _pad_rows_transpose(r16)

    return trip


def _conv_silu(x_ref, xp_ref, w_ref, bias, dst_ref, seq, l2_scale=None, companion=None):
    n = seq // CHUNK
    zeros = jnp.zeros((CONV_HALO, LANES), F32)
    xp_ref[0:CONV_HALO, :] = zeros
    xp_ref[seq + CONV_HALO:seq + 2 * CONV_HALO, :] = zeros

    def copy(ci):
        r0 = pl.multiple_of(ci * CHUNK, CHUNK)
        xp_ref[pl.ds(r0 + CONV_HALO, CHUNK), :] = x_ref[pl.ds(r0, CHUNK), :]
        return
        yield

    _chunk_loop(n, copy)
    w = w_ref[...]

    def conv(ci):
        r0 = pl.multiple_of(ci * CHUNK, CHUNK)
        acc = jnp.zeros((CHUNK, LANES), F32)
        for k in range(CONV_K):
            acc = acc + xp_ref[pl.ds(r0 + (CONV_HALO - CONV_K // 2 + k), CHUNK), :] * w[k:k + 1, :]
        if bias is not None:
            acc = acc + bias
        t = _silu(acc)
        if l2_scale is not None:
            ss = jnp.sum(t * t, axis=-1, keepdims=True)
            yield
            t = t * (lax.rsqrt(ss + RMS_EPS) * l2_scale)
        dst_ref[pl.ds(r0, CHUNK), :] = t
        return
        yield

    if companion is None:
        _chunk_loop(n, conv)
        return

    width = math.gcd(n, DEEP_LOCKSTEP_CHUNKS)

    def conv_series(chunks):
        for ci in chunks:
            yield from conv(ci)
            yield

    def trip(t, carry):
        chunks = [t * width + u for u in range(width)]
        _lockstep([companion(chunks), conv_series(chunks[0::2]), conv_series(chunks[1::2])])
        return carry

    lax.fori_loop(0, n // width, trip, 0)


def _pad_rows_transpose(r16):
    padded = jnp.concatenate([r16, jnp.zeros((CHUNK - r16.shape[0], LANES), F32)], axis=0)
    return padded.T


def _unit_tri_inverse(lmat):
    i = _iota2((CHUNK, CHUNK), 0)
    j = _iota2((CHUNK, CHUNK), 1)

    def same(shift):
        return lax.shift_right_logical(i, shift) == lax.shift_right_logical(j, shift)

    eye = jnp.where(i == j, 1.0, 0.0).astype(F32)
    l16 = jnp.where(same(4), lmat, 0.0)
    x = eye - l16
    p = _mm(l16, l16)
    yield
    for _ in range(2):
        x, p = x + _mm(x, p), _mm(p, p)
        yield
    x = x + _mm(x, p)
    yield
    for shift in (5, 6, 7):
        off = jnp.where(jnp.logical_and(same(shift), jnp.logical_not(same(shift - 1))), lmat, 0.0)
        ex = _mm(off, x)
        yield
        x = x - _mm(x, ex)
        yield
    return x


def _gdn_chunk_terms(gram_kk, gram_qk, q, k, v, rt, r16, d, masks):
    incl, strict = masks[d]
    beta = rt[:, d:d + 1]
    gc_col = rt[:, 2 + d:3 + d]
    gc_row = r16[2 + d:3 + d, :]
    tot_col = rt[:, 10 + d:11 + d]
    decay = jnp.where(incl, jnp.exp(jnp.where(incl, gc_col - gc_row, 0.0)), 0.0)
    lmat = jnp.where(strict, gram_kk * beta * decay, 0.0)
    tinv = yield from _unit_tri_inverse(lmat)
    eg = jnp.exp(gc_col)
    sol = _mm(tinv, jnp.concatenate([v * beta, k * (beta * eg)], axis=1)).astype(BF16)
    yield
    kt = _mm_tn(k * jnp.exp(tot_col - gc_col), sol)
    qs = _mm(gram_qk * decay, sol)
    yield
    return kt[:, LANES:], kt[:, :LANES], q * eg - qs[:, LANES:], qs[:, :LANES]


def _gdn_kernel(q_ref, k_ref, v_ref, z_ref, sm_ref, cwq_ref, cwk_ref, cwv_ref, bias_ref, alog_ref, nw_ref,
                o_ref, xp_ref, qn_ref, kn_ref, vn_ref, rrow_ref, rcol_ref, of_ref, ob_ref, mq_ref, nn_ref,
                tmp_ref, *, seq):
    n = seq // CHUNK
    h = pl.program_id(1)

    row0 = 4 * h
    row = _iota2((SUBLANES, LANES), 0)
    bias8 = _lane_vector_rows(bias_ref[...], tmp_ref.at[0], row0)
    neg_a8 = -jnp.exp(_lane_vector_rows(alog_ref[...], tmp_ref.at[0], row0))

    def gate_values(raw8):
        return jnp.where(row < 2, jax.nn.sigmoid(raw8), neg_a8 * _softplus(raw8 + bias8))

    def gate_rows(vals, prefix, suffix, total):
        return jnp.concatenate([jnp.where(row < 2, vals, jnp.where(row == 2, prefix, suffix)), total], axis=0)

    gates = _gate_rows(sm_ref, tmp_ref, rrow_ref, rcol_ref, row0, gate_values, gate_rows)
    _conv_silu(q_ref, xp_ref, cwq_ref, None, qn_ref, seq, l2_scale=GDN_DK ** -0.5, companion=gates)
    _conv_silu(k_ref, xp_ref, cwk_ref, None, kn_ref, seq, l2_scale=1.0)
    _conv_silu(v_ref, xp_ref, cwv_ref, None, vn_ref, seq)

    masks = _tri_masks()
    outs = (of_ref, ob_ref)

    def local_terms(d, ci):
        q = qn_ref[_rows(ci), :]
        k = kn_ref[_rows(ci), :]
        v = vn_ref[_rows(ci), :]
        rt = rcol_ref[_rows(ci), :]
        r16 = rrow_ref[pl.ds(pl.multiple_of(ci * 16, 16), 16), :]
        grams = _mm_nt(jnp.concatenate([k, q], axis=0), k)
        yield
        m_, n_, q_, o_ = yield from _gdn_chunk_terms(grams[:CHUNK], grams[CHUNK:], q, k, v, rt, r16, d, masks)
        mq_ref[d, pl.ds(pl.multiple_of(2 * ci * CHUNK, CHUNK), 2 * CHUNK), :] = (
            jnp.concatenate([m_, q_], axis=0).astype(BF16))
        nn_ref[d, _rows(ci), :] = n_
        outs[d][_rows(ci), :] = o_

    def advance(d, chunks, state):
        for ci in chunks:
            m = jnp.dot(mq_ref[d, pl.ds(pl.multiple_of(2 * ci * CHUNK, CHUNK), 2 * CHUNK), :], state.astype(BF16),
                        preferred_element_type=F32)
            yield
            outs[d][_rows(ci), :] += m[CHUNK:]
            decay = jnp.exp(jnp.broadcast_to(rrow_ref[pl.ds(ci * 16 + 10 + d, 1), :], (GDN_DK, LANES)))
            state = state * decay - m[:CHUNK] + nn_ref[d, _rows(ci), :]
        return state

    width = math.gcd(n, DEEP_LOCKSTEP_CHUNKS)
    trips = n // width
    sweep_chunks = lambda t: ([t * width + u for u in range(width)], [n - 1 - (t * width + u) for u in range(width)])

    def terms_of(t):
        fwd, bwd = sweep_chunks(t)
        return [local_terms(0, ci) for ci in fwd] + [local_terms(1, ci) for ci in bwd]

    def sweeps_of(t, states):
        fwd, bwd = sweep_chunks(t)
        return [advance(0, fwd, states[0]), advance(1, bwd, states[1])]

    def first_trip(t, states):
        _lockstep(terms_of(t))
        return states

    def middle_trip(t, states):
        return tuple(_lockstep(sweeps_of(t - 1, states) + terms_of(t))[:2])

    def last_sweep(t, states):
        return tuple(_lockstep(sweeps_of(t, states)))

    zero_state = jnp.zeros((GDN_DK, LANES), F32)
    states = lax.fori_loop(0, 1, first_trip, (zero_state, zero_state))
    states = lax.fori_loop(1, trips, middle_trip, states)
    lax.fori_loop(trips - 1, trips, last_sweep, states)

    nw = nw_ref[...]

    def finish(ci):
        o = of_ref[_rows(ci), :] + ob_ref[_rows(ci), :]
        ms = jnp.mean(o * o, axis=-1, keepdims=True)
        yield
        o = o * (lax.rsqrt(ms + RMS_EPS) * nw)
        o_ref[_rows(ci), :] = (o * _silu(z_ref[_rows(ci), :])).astype(o_ref.dtype)

    _chunk_loop(n, finish)


def _proj_block(cb0):
    return lambda seq: pl.BlockSpec((None, seq, LANES), lambda b, h: (cb0 + h, b, 0))


def _gdn(proj, conv_w, bias_vec, alog_vec, norm_w, bsz, seq):
    n = seq // CHUNK
    col = lambda c0: pl.BlockSpec((CONV_K, LANES), lambda b, h: (0, c0 + h))
    vec = pl.BlockSpec((1, LANES), lambda b, h: (0, 0))
    seq_f32 = pltpu.VMEM((seq, LANES), F32)
    return pl.pallas_call(
        functools.partial(_gdn_kernel, seq=seq),
        grid=(bsz, GDN_HEADS),
        in_specs=[
            _proj_block(CB_GDN_Q)(seq), _proj_block(CB_GDN_K)(seq), _proj_block(CB_GDN_V)(seq),
            _proj_block(CB_GDN_Z)(seq),
            pl.BlockSpec((None, seq, LANES), lambda b, h: (CB_SMALL, b, 0)),
            col(0), col(GDN_HEADS), col(2 * GDN_HEADS), vec, vec, vec,
        ],
        out_specs=pl.BlockSpec((seq, LANES), lambda b, h: (b, h)),
        out_shape=jax.ShapeDtypeStruct((bsz * seq, GDN_W), BF16),
        scratch_shapes=[
            pltpu.VMEM((seq + 2 * CONV_HALO, LANES), F32),
            seq_f32, seq_f32, seq_f32,
            pltpu.VMEM((n * 16, LANES), F32),
            seq_f32, seq_f32, seq_f32,
            pltpu.VMEM((2, 2 * seq, LANES), BF16),
            pltpu.VMEM((2, seq, LANES), F32),
            pltpu.VMEM((DEEP_LOCKSTEP_CHUNKS, CHUNK, LANES), F32),
        ],
        compiler_params=_cparams(("parallel", "arbitrary")),
        name="gdn",
    )(proj, proj, proj, proj, proj, conv_w, conv_w, conv_w, bias_vec, alog_vec, norm_w)


def _ret_kernel(lg_ref, q_ref, k_ref, v_ref, g_ref, cos_ref, sin_ref, ng_ref, nb_ref,
                o_ref, of_ref, ob_ref, *, seq):
    n = seq // CHUNK
    width = math.gcd(n, LOCKSTEP_CHUNKS)
    lg = lg_ref[pl.program_id(1)]
    i = _iota2((CHUNK, CHUNK), 0).astype(F32)
    j = _iota2((CHUNK, CHUNK), 1).astype(F32)
    d_sym = jnp.exp(jnp.abs(i - j) * lg)
    q_dec = (jnp.exp((i + 1.0) * lg), jnp.exp((CHUNK - i) * lg))
    k_dec = (jnp.exp((CHUNK - 1.0 - i) * lg), jnp.exp(i * lg))
    chunk_decay = jnp.exp(jnp.full((RET_DK, LANES), CHUNK, F32) * lg)
    outs = (of_ref, ob_ref)

    def rot(t, ci):
        return t * cos_ref[_rows(ci), :] + pltpu.roll(t, LANES // 2, axis=1) * sin_ref[_rows(ci), :]

    def chunk_terms(d, ci):
        q = rot(q_ref[_rows(ci), :], ci)
        k = rot(k_ref[_rows(ci), :], ci) * (RET_DK ** -0.5)
        v = v_ref[_rows(ci), :]
        kv = _mm_tn(k * k_dec[d], v)
        scores = _mm_nt(q, k) * d_sym if d == 0 else None
        yield
        intra = _mm(scores, v) if d == 0 else None
        return q, kv, intra

    def sweep(t, carry):
        chunks = ([t * width + u for u in range(width)], [n - 1 - (t * width + u) for u in range(width)])
        terms = _lockstep([chunk_terms(d, ci) for d in range(2) for ci in chunks[d]])
        states = list(carry)
        for d in range(2):
            for u, ci in enumerate(chunks[d]):
                q, kv, intra = terms[d * width + u]
                inter = _mm(q * q_dec[d], states[d])
                outs[d][_rows(ci), :] = inter if intra is None else intra + inter
                states[d] = states[d] * chunk_decay + kv
        return tuple(states)

    zero_state = jnp.zeros((RET_DK, LANES), F32)
    lax.fori_loop(0, n // width, sweep, (zero_state, zero_state))

    ng = ng_ref[...]
    nb = nb_ref[...]

    def finish(ci):
        o = of_ref[_rows(ci), :] + ob_ref[_rows(ci), :]
        mu = jnp.mean(o, axis=-1, keepdims=True)
        yield
        oc = o - mu
        var = jnp.mean(oc * oc, axis=-1, keepdims=True)
        yield
        o = oc * (lax.rsqrt(var + LN_EPS) * ng) + nb
        o_ref[_rows(ci), :] = (_silu(g_ref[_rows(ci), :]) * o).astype(o_ref.dtype)

    _chunk_loop(n, finish)


def _ret(proj, log_gamma, cos_t, sin_t, norm_g, norm_b, bsz, seq):
    table = pl.BlockSpec((seq, LANES), lambda b, h: (0, 0))
    vec = pl.BlockSpec((1, LANES), lambda b, h: (0, h))
    seq_f32 = pltpu.VMEM((seq, LANES), F32)
    return pl.pallas_call(
        functools.partial(_ret_kernel, seq=seq),
        grid=(bsz, RET_HEADS),
        in_specs=[
            pl.BlockSpec(memory_space=pltpu.SMEM),
            _proj_block(CB_RET_Q)(seq), _proj_block(CB_RET_K)(seq), _proj_block(CB_RET_V)(seq),
            _proj_block(CB_RET_G)(seq),
            table, table, vec, vec,
        ],
        out_specs=pl.BlockSpec((seq, LANES), lambda b, h: (b, h)),
        out_shape=jax.ShapeDtypeStruct((bsz * seq, RET_W), BF16),
        scratch_shapes=[seq_f32, seq_f32],
        compiler_params=_cparams(("parallel", "arbitrary")),
        name="ret",
    )(log_gamma, proj, proj, proj, proj, cos_t, sin_t, norm_g, norm_b)


def _ssd_kernel(x_ref, z_ref, b_ref, c_ref, sm_ref, cwx_ref, cwb_ref, cwc_ref, cbx_ref, cbb_ref, cbc_ref,
                bias_ref, alog_ref, dvec_ref, o_ref, xp_ref, xs_ref, bm_ref, cm_ref, rrow_ref, rcol_ref,
                of_ref, ob_ref, st_ref, sc_ref, tmp_ref, *, seq):
    n = seq // CHUNK
    width = math.gcd(n, LOCKSTEP_CHUNKS)
    p = pl.program_id(1)

    row0 = SMALL_SSD_LANE0 + 4 * p
    row = _iota2((SUBLANES, LANES), 0)

    def rows_twice(x8):
        return jnp.where(row < 4, x8, pltpu.roll(x8, 4, axis=0))

    bias8 = _lane_vector_rows(bias_ref[...], tmp_ref.at[0], row0)
    neg_a8 = rows_twice(-jnp.exp(_lane_vector_rows(alog_ref[...], tmp_ref.at[0], row0)))

    def gate_values(raw8):
        dt = rows_twice(_softplus(raw8 + bias8))
        return jnp.where(row < 4, dt, dt * neg_a8)

    def gate_rows(vals, prefix, suffix, total):
        return jnp.concatenate([jnp.where(row < 4, vals, jnp.where(row < 6, prefix, suffix)), total], axis=0)

    gates = _gate_rows(sm_ref, tmp_ref, rrow_ref, rcol_ref, row0, gate_values, gate_rows)
    _conv_silu(x_ref, xp_ref, cwx_ref, cbx_ref[...], xs_ref, seq, companion=gates)
    _conv_silu(b_ref, xp_ref, cwb_ref, cbb_ref[...], bm_ref, seq)
    _conv_silu(c_ref, xp_ref, cwc_ref, cbc_ref[...], cm_ref, seq)

    masks = _tri_masks()
    first = _iota2((CHUNK, LANES), 1) < SSD_P
    outs = (of_ref, ob_ref)

    def total_rows(ci, d):
        return [jnp.broadcast_to(rrow_ref[pl.ds(ci * 16 + 12 + 2 * d + e, 1), :], (CHUNK, LANES)) for e in range(2)]

    def local_terms(ci):
        x = xs_ref[_rows(ci), :]
        bm = bm_ref[_rows(ci), :]
        cm = cm_ref[_rows(ci), :]
        rt = rcol_ref[_rows(ci), :]
        r16 = rrow_ref[pl.ds(pl.multiple_of(ci * 16, 16), 16), :]
        cb = _mm_nt(cm, bm)
        bm_t = bm.T
        x_heads = jnp.concatenate([jnp.where(first, x, 0.0), jnp.where(first, 0.0, x)], axis=0).astype(BF16)
        yield
        for d in range(2):
            incl = masks[d][0]
            tot = total_rows(ci, d)
            lc_cols, m_parts, g_parts = [], [], []
            for e in range(2):
                dt_row = r16[2 * d + e:2 * d + e + 1, :]
                lc_row = r16[4 + 2 * d + e:5 + 2 * d + e, :]
                lc_col = jnp.broadcast_to(rt[:, 4 + 2 * d + e:5 + 2 * d + e], (CHUNK, LANES))
                seg = jnp.where(incl, jnp.exp(jnp.where(incl, lc_col - lc_row, 0.0)), 0.0)
                m_parts.append(cb * seg * dt_row)
                g_parts.append(bm_t * (dt_row * jnp.exp(tot[e][0:1, :] - lc_row)))
                lc_cols.append(lc_col)
            lhs = jnp.concatenate([jnp.concatenate(m_parts, axis=1), jnp.concatenate(g_parts, axis=1)], axis=0)
            both = jnp.dot(lhs.astype(BF16), x_heads, preferred_element_type=F32)
            outs[d][_rows(ci), :] = both[:CHUNK]
            st_ref[d, _rows(ci), :] = both[CHUNK:]
            sc_ref[d, _rows(ci), :] = jnp.exp(jnp.where(first, lc_cols[0], lc_cols[1]))

    _chunk_loop(n, local_terms)

    def sweep(t, carry):
        chunks = ([t * width + u for u in range(width)], [n - 1 - (t * width + u) for u in range(width)])
        states = list(carry)
        for d in range(2):
            for ci in chunks[d]:
                outs[d][_rows(ci), :] += _mm(cm_ref[_rows(ci), :], states[d]) * sc_ref[d, _rows(ci), :]
                tot = total_rows(ci, d)
                states[d] = states[d] * jnp.exp(jnp.where(first, tot[0], tot[1])) + st_ref[d, _rows(ci), :]
        return tuple(states)

    zero_state = jnp.zeros((SSD_N, 2 * SSD_P), F32)
    lax.fori_loop(0, n // width, sweep, (zero_state, zero_state))

    dvec = dvec_ref[...]

    def finish(ci):
        y = of_ref[_rows(ci), :] + ob_ref[_rows(ci), :] + dvec * xs_ref[_rows(ci), :]
        o_ref[_rows(ci), :] = y * _silu(z_ref[_rows(ci), :])
        return
        yield

    _chunk_loop(n, finish)


def _ssd(proj, conv_w, conv_b, bias_vec, alog_vec, dvec, bsz, seq):
    n = seq // CHUNK
    group = lambda p: p // (SSD_PAIRS // SSD_G)
    wcol = lambda fn: pl.BlockSpec((CONV_K, LANES), lambda b, p: (0, fn(p)))
    bcol = lambda fn: pl.BlockSpec((1, LANES), lambda b, p: (0, fn(p)))
    vec = pl.BlockSpec((1, LANES), lambda b, p: (0, 0))
    seq_f32 = pltpu.VMEM((seq, LANES), F32)
    x_col = lambda p: p
    b_col = lambda p: SSD_W // LANES + group(p)
    c_col = lambda p: SSD_W // LANES + SSD_G + group(p)
    return pl.pallas_call(
        functools.partial(_ssd_kernel, seq=seq),
        grid=(bsz, SSD_PAIRS),
        in_specs=[
            _proj_block(CB_SSD_X)(seq), _proj_block(CB_SSD_Z)(seq),
            pl.BlockSpec((None, seq, LANES), lambda b, p: (CB_SSD_B + group(p), b, 0)),
            pl.BlockSpec((None, seq, LANES), lambda b, p: (CB_SSD_C + group(p), b, 0)),
            pl.BlockSpec((None, seq, LANES), lambda b, p: (CB_SMALL, b, 0)),
            wcol(x_col), wcol(b_col), wcol(c_col), bcol(x_col), bcol(b_col), bcol(c_col),
            vec, vec, bcol(x_col),
        ],
        out_specs=pl.BlockSpec((seq, LANES), lambda b, p: (b, p)),
        out_shape=jax.ShapeDtypeStruct((bsz * seq, SSD_W), F32),
        scratch_shapes=[
            pltpu.VMEM((seq + 2 * CONV_HALO, LANES), F32),
            seq_f32, seq_f32, seq_f32,
            pltpu.VMEM((n * 16, LANES), F32),
            seq_f32, seq_f32, seq_f32,
            pltpu.VMEM((2, seq, LANES), F32),
            pltpu.VMEM((2, seq, LANES), F32),
            pltpu.VMEM((DEEP_LOCKSTEP_CHUNKS, CHUNK, LANES), F32),
        ],
        compiler_params=_cparams(("parallel", "arbitrary")),
        name="ssd",
    )(proj, proj, proj, proj, proj, conv_w, conv_w, conv_w, conv_b, conv_b, conv_b, bias_vec, alog_vec, dvec)


def _out_proj_kernel(x_ref, og_ref, or_ref, y_ref, wg_ref, wr_ref, wy_ref, nw_ref, g_ref, b_ref, o_ref):
    acc = jnp.dot(og_ref[...], wg_ref[...], preferred_element_type=F32)
    acc = acc + jnp.dot(or_ref[...], wr_ref[...], preferred_element_type=F32)
    y = y_ref[...]
    nw = nw_ref[...]
    for g in range(SSD_G):
        lo, hi = g * SSD_GROUP_W, (g + 1) * SSD_GROUP_W
        yg = y[:, lo:hi]
        yg = yg * lax.rsqrt(jnp.mean(yg * yg, axis=-1, keepdims=True) + RMS_EPS) * nw[:, lo:hi]
        acc = acc + jnp.dot(yg.astype(BF16), wy_ref[lo:hi, :], preferred_element_type=F32)
    o_ref[...] = _layer_norm_rows(DEEPNORM_ALPHA * x_ref[...] + acc, g_ref[...], b_ref[...])


def _out_proj(x, o_gdn, o_ret, y_ssd, w_gdn, w_ret, w_ssd, ssd_norm_w, g, b):
    t, d = x.shape
    tm = PROJ_TOKEN_TILE
    rows = lambda width: pl.BlockSpec((tm, width), lambda i: (i, 0))
    whole = lambda a: pl.BlockSpec(a.shape, lambda i: (0, 0))
    return pl.pallas_call(
        _out_proj_kernel,
        grid=(t // tm,),
        in_specs=[
            rows(d), rows(GDN_W), rows(RET_W), rows(SSD_W),
            whole(w_gdn), whole(w_ret), whole(w_ssd), whole(ssd_norm_w), whole(g), whole(b),
        ],
        out_specs=rows(d),
        out_shape=jax.ShapeDtypeStruct((t, d), F32),
        compiler_params=_cparams(("parallel",)),
        name="out_proj",
    )(x, o_gdn, o_ret, y_ssd, w_gdn, w_ret, w_ssd, ssd_norm_w, g, b)


def _small_lane_sources():
    col = -np.ones((LANES,), np.int64)
    par = -np.ones((LANES,), np.int64)
    for h in range(GDN_HEADS):
        for d in range(2):
            col[4 * h + d] = _OFF_GDN_B + d * GDN_HEADS + h
            col[4 * h + 2 + d] = _OFF_GDN_A + d * GDN_HEADS + h
            par[4 * h + 2 + d] = d * GDN_HEADS + h
    for p in range(SSD_PAIRS):
        for d in range(2):
            for e in range(2):
                lane = SMALL_SSD_LANE0 + 4 * p + 2 * d + e
                col[lane] = _OFF_SSD_DT + d * SSD_HEADS + 2 * p + e
                par[lane] = 2 * GDN_HEADS + d * SSD_HEADS + 2 * p + e
    return col, par


def _permute_w_in(w):
    col, _ = _small_lane_sources()
    small = jnp.where(jnp.asarray(col >= 0)[None, :], w[:, np.maximum(col, 0)], 0.0)
    xbc = _OFF_SSD_XBC
    return jnp.concatenate([
        w[:, xbc:xbc + SSD_W],
        w[:, _OFF_SSD_Z:_OFF_SSD_Z + SSD_W],
        w[:, _OFF_GDN_QKV:_OFF_GDN_B],
        w[:, _OFF_RET:_OFF_SSD_XBC],
        w[:, xbc + SSD_W:_OFF_SSD_Z],
        small,
    ], axis=1).astype(BF16)


def _small_lane_vector(gdn_param, ssd_param):
    _, par = _small_lane_sources()
    flat = jnp.concatenate([gdn_param.reshape(-1), ssd_param.reshape(-1)]).astype(F32)
    return jnp.where(jnp.asarray(par >= 0), flat[np.maximum(par, 0)], 0.0).reshape(1, LANES)


def _rotary_tables(seq):
    inv = ROPE_BASE ** (-jnp.arange(0, RET_DK, 2, dtype=F32) / RET_DK)
    ang = jnp.arange(seq, dtype=F32)[:, None] * inv[None, :]
    cos, sin = jnp.cos(ang), jnp.sin(ang)
    return jnp.concatenate([cos, cos], axis=1), jnp.concatenate([-sin, sin], axis=1)


def _trunk(x3, layers):
    bsz, seq, d = x3.shape
    x = x3.reshape(bsz * seq, d)
    cos_t, sin_t = _rotary_tables(seq)
    log_gamma = jnp.log1p(-jnp.exp2(-5.0 - jnp.arange(RET_HEADS, dtype=F32)))
    for p in layers:
        x = _ffn(x, *p["ffn1"], *p["ln1"])
        proj = _in_proj(x, p["w_in"])
        o_gdn = _gdn(proj, p["gdn_conv_w"], p["small_bias"], p["small_alog"], p["gdn_norm_w"], bsz, seq)
        o_ret = _ret(proj, log_gamma, cos_t, sin_t, p["ret_norm_g"], p["ret_norm_b"], bsz, seq)
        y_ssd = _ssd(proj, p["ssd_conv_w"], p["ssd_conv_b"], p["small_bias"], p["small_alog"], p["ssd_d"], bsz, seq)
        x = _out_proj(x, o_gdn, o_ret, y_ssd, *p["w_out"], p["ssd_norm_w"], *p["ln2"])
        x = _ffn(x, *p["ffn2"], *p["ln3"])
    return x.reshape(bsz, seq, d)


def kernel(x_prompt, x_sample, w_in, w_out, gdn_conv_w, gdn_a_log, gdn_dt_bias, gdn_norm_w, ret_norm_g, ret_norm_b,
           ssd_conv_w, ssd_conv_b, ssd_a_log, ssd_dt_bias, ssd_d, ssd_norm_w,
           ffn1_w_gate, ffn1_w_up, ffn1_w_down, ffn2_w_gate, ffn2_w_up, ffn2_w_down,
           ln1_g, ln1_b, ln2_g, ln2_b, ln3_g, ln3_b):
    row = lambda a: a.astype(F32).reshape(1, -1)
    layers = []
    for l in range(DEPTH):
        wo = w_out[l].astype(BF16)
        layers.append(dict(
            ffn1=(ffn1_w_gate[l].astype(BF16), ffn1_w_up[l].astype(BF16), ffn1_w_down[l].astype(BF16)),
            ffn2=(ffn2_w_gate[l].astype(BF16), ffn2_w_up[l].astype(BF16), ffn2_w_down[l].astype(BF16)),
            ln1=(row(ln1_g[l]), row(ln1_b[l])),
            ln2=(row(ln2_g[l]), row(ln2_b[l])),
            ln3=(row(ln3_g[l]), row(ln3_b[l])),
            w_in=_permute_w_in(w_in[l]),
            w_out=(wo[:GDN_W], wo[GDN_W:GDN_W + RET_W], wo[GDN_W + RET_W:]),
            gdn_conv_w=gdn_conv_w[l].astype(F32),
            small_bias=_small_lane_vector(gdn_dt_bias[l], ssd_dt_bias[l]),
            small_alog=_small_lane_vector(gdn_a_log[l], ssd_a_log[l]),
            gdn_norm_w=row(gdn_norm_w[l]),
            ret_norm_g=row(ret_norm_g[l]),
            ret_norm_b=row(ret_norm_b[l]),
            ssd_conv_w=ssd_conv_w[l].astype(F32),
            ssd_conv_b=row(ssd_conv_b[l]),
            ssd_d=jnp.repeat(ssd_d[l].astype(F32), SSD_P).reshape(1, SSD_W),
            ssd_norm_w=row(ssd_norm_w[l]),
        ))
    return _trunk(x_prompt, layers), _trunk(x_sample, layers)
```

```python
import functools
import math

import jax
import jax.numpy as jnp
import numpy as np
from jax import lax
from jax.experimental import pallas as pl
from jax.experimental.pallas import tpu as pltpu

F32 = jnp.float32
BF16 = jnp.bfloat16

D_MODEL = 2048
DEPTH = 2
CONV_K = 5
GDN_HEADS = 6
GDN_DK = 128
RET_HEADS = 4
RET_DK = 128
SSD_HEADS = 12
SSD_P = 64
SSD_N = 128
SSD_G = 2
D_FF = 5632
ROPE_BASE = 10000.0
LN_EPS = 1e-5
RMS_EPS = 1e-6
DEEPNORM_ALPHA = (2 * DEPTH) ** 0.25

GDN_W = GDN_HEADS * 128
RET_W = RET_HEADS * 128
SSD_W = SSD_HEADS * SSD_P
SSD_GROUP_W = SSD_W // SSD_G
SSD_PAIRS = SSD_HEADS // 2

LANES = 128
SUBLANES = 8
VMEM_LIMIT_BYTES = 60000 * 1024

CHUNK = 128
CONV_HALO = 8
LOCKSTEP_CHUNKS = 4
DEEP_LOCKSTEP_CHUNKS = 8

CB_SSD_X = 0
CB_SSD_Z = 6
CB_GDN_Q = 12
CB_GDN_K = 18
CB_GDN_V = 24
CB_GDN_Z = 30
CB_RET_Q = 36
CB_RET_K = 40
CB_RET_V = 44
CB_RET_G = 48
CB_SSD_B = 52
CB_SSD_C = 54
CB_SMALL = 56
N_COL_BLOCKS = 57
IN_PROJ_COL_GROUPS = 3
COL_BLOCKS_PER_GROUP = N_COL_BLOCKS // IN_PROJ_COL_GROUPS
SMALL_GDN_LANES = 4 * GDN_HEADS
SMALL_SSD_LANE0 = 32
SMALL_SSD_LANES = 4 * SSD_PAIRS

_OFF_GDN_QKV = 0
_OFF_GDN_Z = 2304
_OFF_GDN_B = 3072
_OFF_GDN_A = 3084
_OFF_RET = 3096
_OFF_SSD_XBC = 5144
_OFF_SSD_Z = 6424
_OFF_SSD_DT = 7192

FFN_TOKEN_TILE = 1024
FFN_FF_TILE = 512
FFN_LAST_STEP_ROW_CHUNKS = 4
PROJ_TOKEN_TILE = 512
OUT_PROJ_ROW_CHUNKS = 2


def _cparams(semantics):
    return pltpu.CompilerParams(dimension_semantics=semantics, vmem_limit_bytes=VMEM_LIMIT_BYTES)


def _mm(a, b):
    return jnp.dot(a.astype(BF16), b.astype(BF16), preferred_element_type=F32)


def _mm_nt(a, b):
    return lax.dot_general(a.astype(BF16), b.astype(BF16), (((1,), (1,)), ((), ())), preferred_element_type=F32)


def _mm_tn(a, b):
    return lax.dot_general(a.astype(BF16), b.astype(BF16), (((0,), (0,)), ((), ())), preferred_element_type=F32)


def _silu(x):
    return x * jax.nn.sigmoid(x)


def _softplus(x):
    return jnp.maximum(x, 0.0) + jnp.log1p(jnp.exp(-jnp.abs(x)))


def _layer_norm_rows(y, g, b):
    mu = jnp.mean(y, axis=-1, keepdims=True)
    yc = y - mu
    var = jnp.mean(yc * yc, axis=-1, keepdims=True)
    return yc * lax.rsqrt(var + LN_EPS) * g + b


def _ffn_kernel(x_ref, wg_ref, wu_ref, wd_ref, g_ref, b_ref, o_ref):
    j = pl.program_id(1)
    last = j == pl.num_programs(1) - 1

    @pl.when(j == 0)
    def _():
        o_ref[...] = jnp.zeros_like(o_ref)

    def partial_sum(rows):
        xb = x_ref[rows, :].astype(BF16)
        gate = jnp.dot(xb, wg_ref[...], preferred_element_type=F32)
        up = jnp.dot(xb, wu_ref[...], preferred_element_type=F32)
        h = (_silu(gate) * up).astype(BF16)
        return jnp.dot(h, wd_ref[...], preferred_element_type=F32)

    @pl.when(jnp.logical_not(last))
    def _():
        o_ref[...] += partial_sum(slice(None))

    @pl.when(last)
    def _():
        chunk = o_ref.shape[0] // FFN_LAST_STEP_ROW_CHUNKS
        for r in range(FFN_LAST_STEP_ROW_CHUNKS):
            rows = slice(r * chunk, (r + 1) * chunk)
            y = DEEPNORM_ALPHA * x_ref[rows, :] + 0.5 * (o_ref[rows, :] + partial_sum(rows))
            o_ref[rows, :] = _layer_norm_rows(y, g_ref[...], b_ref[...])


def _ffn(x, wg, wu, wd, g, b):
    t, d = x.shape
    f = wg.shape[1]
    tm, tf = FFN_TOKEN_TILE, FFN_FF_TILE
    return pl.pallas_call(
        _ffn_kernel,
        grid=(t // tm, f // tf),
        in_specs=[
            pl.BlockSpec((tm, d), lambda i, j: (i, 0)),
            pl.BlockSpec((d, tf), lambda i, j: (0, j)),
            pl.BlockSpec((d, tf), lambda i, j: (0, j)),
            pl.BlockSpec((tf, d), lambda i, j: (j, 0)),
            pl.BlockSpec((1, d), lambda i, j: (0, 0)),
            pl.BlockSpec((1, d), lambda i, j: (0, 0)),
        ],
        out_specs=pl.BlockSpec((tm, d), lambda i, j: (i, 0)),
        out_shape=jax.ShapeDtypeStruct((t, d), F32),
        compiler_params=_cparams(("parallel", "arbitrary")),
        name="ffn",
    )(x, wg, wu, wd, g, b)


def _in_proj_kernel(x_ref, w_ref, o_ref):
    r = jnp.dot(x_ref[...].astype(BF16), w_ref[...], preferred_element_type=F32)
    for c in range(COL_BLOCKS_PER_GROUP):
        o_ref[c] = r[:, c * LANES:(c + 1) * LANES]


def _in_proj(x, w):
    t, d = x.shape
    tm = PROJ_TOKEN_TILE
    tn = COL_BLOCKS_PER_GROUP * LANES
    return pl.pallas_call(
        _in_proj_kernel,
        grid=(IN_PROJ_COL_GROUPS, t // tm),
        in_specs=[
            pl.BlockSpec((tm, d), lambda j, i: (i, 0)),
            pl.BlockSpec((d, tn), lambda j, i: (0, j)),
        ],
        out_specs=pl.BlockSpec((COL_BLOCKS_PER_GROUP, tm, LANES), lambda j, i: (j, i, 0)),
        out_shape=jax.ShapeDtypeStruct((N_COL_BLOCKS, t, LANES), F32),
        compiler_params=_cparams(("arbitrary", "arbitrary")),
        name="in_proj",
    )(x, w)


def _lockstep(gens):
    results = [None] * len(gens)
    live = list(enumerate(gens))
    while live:
        still = []
        for idx, gen in live:
            try:
                next(gen)
                still.append((idx, gen))
            except StopIteration as stop:
                results[idx] = stop.value
        live = still
    return results


def _chunk_loop(n, chunk_gen):
    width = math.gcd(n, LOCKSTEP_CHUNKS)

    def trip(t, carry):
        _lockstep([chunk_gen(t * width + u) for u in range(width)])
        return carry

    lax.fori_loop(0, n // width, trip, 0)


def _rows(ci):
    return pl.ds(pl.multiple_of(ci * CHUNK, CHUNK), CHUNK)


def _iota2(shape, axis):
    return lax.broadcasted_iota(jnp.int32, shape, axis)


def _tri_masks():
    i = _iota2((CHUNK, CHUNK), 0)
    j = _iota2((CHUNK, CHUNK), 1)
    return ((j <= i, j < i), (j >= i, j > i))


def _cumsum_mat():
    j = _iota2((CHUNK, 3 * CHUNK), 0)
    i = _iota2((CHUNK, 3 * CHUNK), 1)
    prefix = jnp.logical_and(i < CHUNK, j <= i)
    suffix_or_total = jnp.logical_and(i >= CHUNK, jnp.logical_or(j + CHUNK >= i, i >= 2 * CHUNK))
    return jnp.where(jnp.logical_or(prefix, suffix_or_total), 1.0, 0.0).astype(BF16)


def _chunk_sums(rows_list, cmat):
    hi = [r.astype(BF16).astype(F32) for r in rows_list]
    mid = [(r - h).astype(BF16).astype(F32) for r, h in zip(rows_list, hi)]
    lo = [r - h - m for r, h, m in zip(rows_list, hi, mid)]
    sums = jnp.dot(jnp.concatenate(hi + mid + lo, axis=0).astype(BF16), cmat, preferred_element_type=F32)
    k = len(rows_list)
    out = []
    for u in range(k):
        s = sum(sums[(part * k + u) * SUBLANES:(part * k + u + 1) * SUBLANES] for part in range(3))
        out.append((s[:, :CHUNK], s[:, CHUNK:2 * CHUNK], s[:, 2 * CHUNK:]))
    return out


def _lane_vector_rows(vec, tmp_ref, row0):
    tmp_ref[...] = jnp.broadcast_to(vec, (CHUNK, LANES)).T
    return tmp_ref[pl.ds(row0, SUBLANES), :]


def _gate_rows(sm_ref, tmp_ref, rrow_ref, rcol_ref, row0, row_values, assemble):
    cmat = _cumsum_mat()

    def trip(chunks):
        for u, ci in enumerate(chunks):
            tmp_ref[u] = sm_ref[_rows(ci), :].T
        yield
        rows = [row_values(tmp_ref[u, pl.ds(row0, SUBLANES), :]) for u in range(len(chunks))]
        yield
        sums = _chunk_sums(rows, cmat)
        yield
        r16s = [assemble(r, *s) for r, s in zip(rows, sums)]
        for ci, r16 in zip(chunks, r16s):
            rrow_ref[pl.ds(pl.multiple_of(ci * 16, 16), 16), :] = r16
        for ci, r16 in zip(chunks, r16s):
            rcol_ref[_rows(ci), :] = _pad_rows_transpose(r16)

    return trip


def _conv_silu(x_ref, xp_ref, w_ref, bias, dst_ref, seq, l2_scale=None, companion=None):
    n = seq // CHUNK
    zeros = jnp.zeros((CONV_HALO, LANES), F32)
    xp_ref[0:CONV_HALO, :] = zeros
    xp_ref[seq + CONV_HALO:seq + 2 * CONV_HALO, :] = zeros

    def copy(ci):
        r0 = pl.multiple_of(ci * CHUNK, CHUNK)
        xp_ref[pl.ds(r0 + CONV_HALO, CHUNK), :] = x_ref[pl.ds(r0, CHUNK), :]
        return
        yield

    _chunk_loop(n, copy)
    w = w_ref[...]

    def conv(ci):
        r0 = pl.multiple_of(ci * CHUNK, CHUNK)
        acc = jnp.zeros((CHUNK, LANES), F32)
        for k in range(CONV_K):
            acc = acc + xp_ref[pl.ds(r0 + (CONV_HALO - CONV_K // 2 + k), CHUNK), :] * w[k:k + 1, :]
        if bias is not None:
            acc = acc + bias
        t = _silu(acc)
        if l2_scale is not None:
            ss = jnp.sum(t * t, axis=-1, keepdims=True)
            yield
            t = t * (lax.rsqrt(ss + RMS_EPS) * l2_scale)
        dst_ref[pl.ds(r0, CHUNK), :] = t
        return
        yield

    if companion is None:
        _chunk_loop(n, conv)
        return

    width = math.gcd(n, DEEP_LOCKSTEP_CHUNKS)

    def conv_series(chunks):
        for ci in chunks:
            yield from conv(ci)
            yield

    def trip(t, carry):
        chunks = [t * width + u for u in range(width)]
        _lockstep([companion(chunks), conv_series(chunks[0::2]), conv_series(chunks[1::2])])
        return carry

    lax.fori_loop(0, n // width, trip, 0)


def _pad_rows_transpose(r16):
    padded = jnp.concatenate([r16, jnp.zeros((CHUNK - r16.shape[0], LANES), F32)], axis=0)
    return padded.T


def _unit_tri_inverse(lmat):
    i = _iota2((CHUNK, CHUNK), 0)
    j = _iota2((CHUNK, CHUNK), 1)

    def same(shift):
        return lax.shift_right_logical(i, shift) == lax.shift_right_logical(j, shift)

    eye = jnp.where(i == j, 1.0, 0.0).astype(F32)
    l16 = jnp.where(same(4), lmat, 0.0)
    x = eye - l16
    p = _mm(l16, l16)
    yield
    for _ in range(2):
        x, p = x + _mm(x, p), _mm(p, p)
        yield
    x = x + _mm(x, p)
    yield
    for shift in (5, 6, 7):
        off = jnp.where(jnp.logical_and(same(shift), jnp.logical_not(same(shift - 1))), lmat, 0.0)
        ex = _mm(off, x)
        yield
        x = x - _mm(x, ex)
        yield
    return x


def _gdn_chunk_terms(gram_kk, gram_qk, q, k, v, rt, r16, d, masks):
    incl, strict = masks[d]
    beta = rt[:, d:d + 1]
    gc_col = rt[:, 2 + d:3 + d]
    gc_row = r16[2 + d:3 + d, :]
    tot_col = rt[:, 10 + d:11 + d]
    decay = jnp.where(incl, jnp.exp(jnp.where(incl, gc_col - gc_row, 0.0)), 0.0)
    lmat = jnp.where(strict, gram_kk * beta * decay, 0.0)
    tinv = yield from _unit_tri_inverse(lmat)
    eg = jnp.exp(gc_col)
    sol = _mm(tinv, jnp.concatenate([v * beta, k * (beta * eg)], axis=1)).astype(BF16)
    yield
    kt = _mm_tn(k * jnp.exp(tot_col - gc_col), sol)
    qs = _mm(gram_qk * decay, sol)
    yield
    return kt[:, LANES:], kt[:, :LANES], q * eg - qs[:, LANES:], qs[:, :LANES]


def _gdn_kernel(q_ref, k_ref, v_ref, z_ref, sm_ref, cwq_ref, cwk_ref, cwv_ref, bias_ref, alog_ref, nw_ref,
                o_ref, xp_ref, qn_ref, kn_ref, vn_ref, rrow_ref, rcol_ref, of_ref, ob_ref, mq_ref, nn_ref,
                tmp_ref, *, seq):
    n = seq // CHUNK
    h = pl.program_id(1)

    row0 = 4 * h
    row = _iota2((SUBLANES, LANES), 0)
    bias8 = _lane_vector_rows(bias_ref[...], tmp_ref.at[0], row0)
    neg_a8 = -jnp.exp(_lane_vector_rows(alog_ref[...], tmp_ref.at[0], row0))

    def gate_values(raw8):
        return jnp.where(row < 2, jax.nn.sigmoid(raw8), neg_a8 * _softplus(raw8 + bias8))

    def gate_rows(vals, prefix, suffix, total):
        return jnp.concatenate([jnp.where(row < 2, vals, jnp.where(row == 2, prefix, suffix)), total], axis=0)

    gates = _gate_rows(sm_ref, tmp_ref, rrow_ref, rcol_ref, row0, gate_values, gate_rows)
    _conv_silu(q_ref, xp_ref, cwq_ref, None, qn_ref, seq, l2_scale=GDN_DK ** -0.5, companion=gates)
    _conv_silu(k_ref, xp_ref, cwk_ref, None, kn_ref, seq, l2_scale=1.0)
    _conv_silu(v_ref, xp_ref, cwv_ref, None, vn_ref, seq)

    masks = _tri_masks()
    outs = (of_ref, ob_ref)

    def local_terms(d, ci):
        q = qn_ref[_rows(ci), :]
        k = kn_ref[_rows(ci), :]
        v = vn_ref[_rows(ci), :]
        rt = rcol_ref[_rows(ci), :]
        r16 = rrow_ref[pl.ds(pl.multiple_of(ci * 16, 16), 16), :]
        grams = _mm_nt(jnp.concatenate([k, q], axis=0), k)
        yield
        m_, n_, q_, o_ = yield from _gdn_chunk_terms(grams[:CHUNK], grams[CHUNK:], q, k, v, rt, r16, d, masks)
        mq_ref[d, pl.ds(pl.multiple_of(2 * ci * CHUNK, CHUNK), 2 * CHUNK), :] = (
            jnp.concatenate([m_, q_], axis=0).astype(BF16))
        nn_ref[d, _rows(ci), :] = n_
        outs[d][_rows(ci), :] = o_

    def advance(d, chunks, state):
        for ci in chunks:
            m = jnp.dot(mq_ref[d, pl.ds(pl.multiple_of(2 * ci * CHUNK, CHUNK), 2 * CHUNK), :], state.astype(BF16),
                        preferred_element_type=F32)
            yield
            outs[d][_rows(ci), :] += m[CHUNK:]
            decay = jnp.exp(jnp.broadcast_to(rrow_ref[pl.ds(ci * 16 + 10 + d, 1), :], (GDN_DK, LANES)))
            state = state * decay - m[:CHUNK] + nn_ref[d, _rows(ci), :]
        return state

    width = math.gcd(n, DEEP_LOCKSTEP_CHUNKS)
    trips = n // width
    sweep_chunks = lambda t: ([t * width + u for u in range(width)], [n - 1 - (t * width + u) for u in range(width)])

    def terms_of(t):
        fwd, bwd = sweep_chunks(t)
        return [local_terms(0, ci) for ci in fwd] + [local_terms(1, ci) for ci in bwd]

    def sweeps_of(t, states):
        fwd, bwd = sweep_chunks(t)
        return [advance(0, fwd, states[0]), advance(1, bwd, states[1])]

    def first_trip(t, states):
        _lockstep(terms_of(t))
        return states

    def middle_trip(t, states):
        return tuple(_lockstep(sweeps_of(t - 1, states) + terms_of(t))[:2])

    def last_sweep(t, states):
        return tuple(_lockstep(sweeps_of(t, states)))

    zero_state = jnp.zeros((GDN_DK, LANES), F32)
    states = lax.fori_loop(0, 1, first_trip, (zero_state, zero_state))
    states = lax.fori_loop(1, trips, middle_trip, states)
    lax.fori_loop(trips - 1, trips, last_sweep, states)

    nw = nw_ref[...]

    def finish(ci):
        o = of_ref[_rows(ci), :] + ob_ref[_rows(ci), :]
        ms = jnp.mean(o * o, axis=-1, keepdims=True)
        yield
        o = o * (lax.rsqrt(ms + RMS_EPS) * nw)
        o_ref[_rows(ci), :] = (o * _silu(z_ref[_rows(ci), :])).astype(o_ref.dtype)

    _chunk_loop(n, finish)


def _proj_block(cb0):
    return lambda seq: pl.BlockSpec((None, seq, LANES), lambda b, h: (cb0 + h, b, 0))


def _gdn(proj, conv_w, bias_vec, alog_vec, norm_w, bsz, seq):
    n = seq // CHUNK
    col = lambda c0: pl.BlockSpec((CONV_K, LANES), lambda b, h: (0, c0 + h))
    vec = pl.BlockSpec((1, LANES), lambda b, h: (0, 0))
    seq_f32 = pltpu.VMEM((seq, LANES), F32)
    return pl.pallas_call(
        functools.partial(_gdn_kernel, seq=seq),
        grid=(bsz, GDN_HEADS),
        in_specs=[
            _proj_block(CB_GDN_Q)(seq), _proj_block(CB_GDN_K)(seq), _proj_block(CB_GDN_V)(seq),
            _proj_block(CB_GDN_Z)(seq),
            pl.BlockSpec((None, seq, LANES), lambda b, h: (CB_SMALL, b, 0)),
            col(0), col(GDN_HEADS), col(2 * GDN_HEADS), vec, vec, vec,
        ],
        out_specs=pl.BlockSpec((seq, LANES), lambda b, h: (b, h)),
        out_shape=jax.ShapeDtypeStruct((bsz * seq, GDN_W), BF16),
        scratch_shapes=[
            pltpu.VMEM((seq + 2 * CONV_HALO, LANES), F32),
            seq_f32, seq_f32, seq_f32,
            pltpu.VMEM((n * 16, LANES), F32),
            seq_f32, seq_f32, seq_f32,
            pltpu.VMEM((2, 2 * seq, LANES), BF16),
            pltpu.VMEM((2, seq, LANES), F32),
            pltpu.VMEM((DEEP_LOCKSTEP_CHUNKS, CHUNK, LANES), F32),
        ],
        compiler_params=_cparams(("parallel", "arbitrary")),
        name="gdn",
    )(proj, proj, proj, proj, proj, conv_w, conv_w, conv_w, bias_vec, alog_vec, norm_w)


def _ret_kernel(lg_ref, q_ref, k_ref, v_ref, g_ref, cos_ref, sin_ref, ng_ref, nb_ref,
                o_ref, of_ref, ob_ref, *, seq):
    n = seq // CHUNK
    width = math.gcd(n, LOCKSTEP_CHUNKS)
    lg = lg_ref[pl.program_id(1)]
    i = _iota2((CHUNK, CHUNK), 0).astype(F32)
    j = _iota2((CHUNK, CHUNK), 1).astype(F32)
    d_sym = jnp.exp(jnp.abs(i - j) * lg)
    q_dec = (jnp.exp((i + 1.0) * lg), jnp.exp((CHUNK - i) * lg))
    k_dec = (jnp.exp((CHUNK - 1.0 - i) * lg), jnp.exp(i * lg))
    chunk_decay = jnp.exp(jnp.full((RET_DK, LANES), CHUNK, F32) * lg)
    outs = (of_ref, ob_ref)

    def rot(t, ci):
        return t * cos_ref[_rows(ci), :] + pltpu.roll(t, LANES // 2, axis=1) * sin_ref[_rows(ci), :]

    def chunk_terms(d, ci):
        q = rot(q_ref[_rows(ci), :], ci)
        k = rot(k_ref[_rows(ci), :], ci) * (RET_DK ** -0.5)
        v = v_ref[_rows(ci), :]
        kv = _mm_tn(k * k_dec[d], v)
        scores = _mm_nt(q, k) * d_sym if d == 0 else None
        yield
        intra = _mm(scores, v) if d == 0 else None
        return q, kv, intra

    def sweep(t, carry):
        chunks = ([t * width + u for u in range(width)], [n - 1 - (t * width + u) for u in range(width)])
        terms = _lockstep([chunk_terms(d, ci) for d in range(2) for ci in chunks[d]])
        states = list(carry)
        for d in range(2):
            for u, ci in enumerate(chunks[d]):
                q, kv, intra = terms[d * width + u]
                inter = _mm(q * q_dec[d], states[d])
                outs[d][_rows(ci), :] = inter if intra is None else intra + inter
                states[d] = states[d] * chunk_decay + kv
        return tuple(states)

    zero_state = jnp.zeros((RET_DK, LANES), F32)
    lax.fori_loop(0, n // width, sweep, (zero_state, zero_state))

    ng = ng_ref[...]
    nb = nb_ref[...]

    def finish(ci):
        o = of_ref[_rows(ci), :] + ob_ref[_rows(ci), :]
        mu = jnp.mean(o, axis=-1, keepdims=True)
        yield
        oc = o - mu
        var = jnp.mean(oc * oc, axis=-1, keepdims=True)
        yield
        o = oc * (lax.rsqrt(var + LN_EPS) * ng) + nb
        o_ref[_rows(ci), :] = (_silu(g_ref[_rows(ci), :]) * o).astype(o_ref.dtype)

    _chunk_loop(n, finish)


def _ret(proj, log_gamma, cos_t, sin_t, norm_g, norm_b, bsz, seq):
    table = pl.BlockSpec((seq, LANES), lambda b, h: (0, 0))
    vec = pl.BlockSpec((1, LANES), lambda b, h: (0, h))
    seq_f32 = pltpu.VMEM((seq, LANES), F32)
    return pl.pallas_call(
        functools.partial(_ret_kernel, seq=seq),
        grid=(bsz, RET_HEADS),
        in_specs=[
            pl.BlockSpec(memory_space=pltpu.SMEM),
            _proj_block(CB_RET_Q)(seq), _proj_block(CB_RET_K)(seq), _proj_block(CB_RET_V)(seq),
            _proj_block(CB_RET_G)(seq),
            table, table, vec, vec,
        ],
        out_specs=pl.BlockSpec((seq, LANES), lambda b, h: (b, h)),
        out_shape=jax.ShapeDtypeStruct((bsz * seq, RET_W), BF16),
        scratch_shapes=[seq_f32, seq_f32],
        compiler_params=_cparams(("parallel", "arbitrary")),
        name="ret",
    )(log_gamma, proj, proj, proj, proj, cos_t, sin_t, norm_g, norm_b)


def _ssd_kernel(x_ref, z_ref, b_ref, c_ref, sm_ref, cwx_ref, cwb_ref, cwc_ref, cbx_ref, cbb_ref, cbc_ref,
                bias_ref, alog_ref, dvec_ref, o_ref, xp_ref, xs_ref, bm_ref, cm_ref, rrow_ref, rcol_ref,
                of_ref, ob_ref, st_ref, sc_ref, tmp_ref, *, seq):
    n = seq // CHUNK
    width = math.gcd(n, LOCKSTEP_CHUNKS)
    p = pl.program_id(1)

    row0 = SMALL_SSD_LANE0 + 4 * p
    row = _iota2((SUBLANES, LANES), 0)

    def rows_twice(x8):
        return jnp.where(row < 4, x8, pltpu.roll(x8, 4, axis=0))

    bias8 = _lane_vector_rows(bias_ref[...], tmp_ref.at[0], row0)
    neg_a8 = rows_twice(-jnp.exp(_lane_vector_rows(alog_ref[...], tmp_ref.at[0], row0)))

    def gate_values(raw8):
        dt = rows_twice(_softplus(raw8 + bias8))
        return jnp.where(row < 4, dt, dt * neg_a8)

    def gate_rows(vals, prefix, suffix, total):
        return jnp.concatenate([jnp.where(row < 4, vals, jnp.where(row < 6, prefix, suffix)), total], axis=0)

    gates = _gate_rows(sm_ref, tmp_ref, rrow_ref, rcol_ref, row0, gate_values, gate_rows)
    _conv_silu(x_ref, xp_ref, cwx_ref, cbx_ref[...], xs_ref, seq, companion=gates)
    _conv_silu(b_ref, xp_ref, cwb_ref, cbb_ref[...], bm_ref, seq)
    _conv_silu(c_ref, xp_ref, cwc_ref, cbc_ref[...], cm_ref, seq)

    masks = _tri_masks()
    first = _iota2((CHUNK, LANES), 1) < SSD_P
    outs = (of_ref, ob_ref)

    def total_rows(ci, d):
        return [jnp.broadcast_to(rrow_ref[pl.ds(ci * 16 + 12 + 2 * d + e, 1), :], (CHUNK, LANES)) for e in range(2)]

    def local_terms(ci):
        x = xs_ref[_rows(ci), :]
        bm = bm_ref[_rows(ci), :]
        cm = cm_ref[_rows(ci), :]
        rt = rcol_ref[_rows(ci), :]
        r16 = rrow_ref[pl.ds(pl.multiple_of(ci * 16, 16), 16), :]
        cb = _mm_nt(cm, bm)
        bm_t = bm.T
        x_heads = jnp.concatenate([jnp.where(first, x, 0.0), jnp.where(first, 0.0, x)], axis=0).astype(BF16)
        yield
        for d in range(2):
            incl = masks[d][0]
            tot = total_rows(ci, d)
            lc_cols, m_parts, g_parts = [], [], []
            for e in range(2):
                dt_row = r16[2 * d + e:2 * d + e + 1, :]
                lc_row = r16[4 + 2 * d + e:5 + 2 * d + e, :]
                lc_col = jnp.broadcast_to(rt[:, 4 + 2 * d + e:5 + 2 * d + e], (CHUNK, LANES))
                seg = jnp.where(incl, jnp.exp(jnp.where(incl, lc_col - lc_row, 0.0)), 0.0)
                m_parts.append(cb * seg * dt_row)
                g_parts.append(bm_t * (dt_row * jnp.exp(tot[e][0:1, :] - lc_row)))
                lc_cols.append(lc_col)
            lhs = jnp.concatenate([jnp.concatenate(m_parts, axis=1), jnp.concatenate(g_parts, axis=1)], axis=0)
            both = jnp.dot(lhs.astype(BF16), x_heads, preferred_element_type=F32)
            outs[d][_rows(ci), :] = both[:CHUNK]
            st_ref[d, _rows(ci), :] = both[CHUNK:]
            sc_ref[d, _rows(ci), :] = jnp.exp(jnp.where(first, lc_cols[0], lc_cols[1]))

    _chunk_loop(n, local_terms)

    def sweep(t, carry):
        chunks = ([t * width + u for u in range(width)], [n - 1 - (t * width + u) for u in range(width)])
        states = list(carry)
        for d in range(2):
            for ci in chunks[d]:
                outs[d][_rows(ci), :] += _mm(cm_ref[_rows(ci), :], states[d]) * sc_ref[d, _rows(ci), :]
                tot = total_rows(ci, d)
                states[d] = states[d] * jnp.exp(jnp.where(first, tot[0], tot[1])) + st_ref[d, _rows(ci), :]
        return tuple(states)

    zero_state = jnp.zeros((SSD_N, 2 * SSD_P), F32)
    lax.fori_loop(0, n // width, sweep, (zero_state, zero_state))

    dvec = dvec_ref[...]

    def finish(ci):
        y = of_ref[_rows(ci), :] + ob_ref[_rows(ci), :] + dvec * xs_ref[_rows(ci), :]
        o_ref[_rows(ci), :] = y * _silu(z_ref[_rows(ci), :])
        return
        yield

    _chunk_loop(n, finish)


def _ssd(proj, conv_w, conv_b, bias_vec, alog_vec, dvec, bsz, seq):
    n = seq // CHUNK
    group = lambda p: p // (SSD_PAIRS // SSD_G)
    wcol = lambda fn: pl.BlockSpec((CONV_K, LANES), lambda b, p: (0, fn(p)))
    bcol = lambda fn: pl.BlockSpec((1, LANES), lambda b, p: (0, fn(p)))
    vec = pl.BlockSpec((1, LANES), lambda b, p: (0, 0))
    seq_f32 = pltpu.VMEM((seq, LANES), F32)
    x_col = lambda p: p
    b_col = lambda p: SSD_W // LANES + group(p)
    c_col = lambda p: SSD_W // LANES + SSD_G + group(p)
    return pl.pallas_call(
        functools.partial(_ssd_kernel, seq=seq),
        grid=(bsz, SSD_PAIRS),
        in_specs=[
            _proj_block(CB_SSD_X)(seq), _proj_block(CB_SSD_Z)(seq),
            pl.BlockSpec((None, seq, LANES), lambda b, p: (CB_SSD_B + group(p), b, 0)),
            pl.BlockSpec((None, seq, LANES), lambda b, p: (CB_SSD_C + group(p), b, 0)),
            pl.BlockSpec((None, seq, LANES), lambda b, p: (CB_SMALL, b, 0)),
            wcol(x_col), wcol(b_col), wcol(c_col), bcol(x_col), bcol(b_col), bcol(c_col),
            vec, vec, bcol(x_col),
        ],
        out_specs=pl.BlockSpec((seq, LANES), lambda b, p: (b, p)),
        out_shape=jax.ShapeDtypeStruct((bsz * seq, SSD_W), F32),
        scratch_shapes=[
            pltpu.VMEM((seq + 2 * CONV_HALO, LANES), F32),
            seq_f32, seq_f32, seq_f32,
            pltpu.VMEM((n * 16, LANES), F32),
            seq_f32, seq_f32, seq_f32,
            pltpu.VMEM((2, seq, LANES), F32),
            pltpu.VMEM((2, seq, LANES), F32),
            pltpu.VMEM((DEEP_LOCKSTEP_CHUNKS, CHUNK, LANES), F32),
        ],
        compiler_params=_cparams(("parallel", "arbitrary")),
        name="ssd",
    )(proj, proj, proj, proj, proj, conv_w, conv_w, conv_w, conv_b, conv_b, conv_b, bias_vec, alog_vec, dvec)


def _out_proj_kernel(x_ref, og_ref, or_ref, y_ref, wg_ref, wr_ref, wy_ref, nw_ref, g_ref, b_ref, o_ref):
    nw = nw_ref[...]
    chunk = o_ref.shape[0] // OUT_PROJ_ROW_CHUNKS
    for r in range(OUT_PROJ_ROW_CHUNKS):
        rows = slice(r * chunk, (r + 1) * chunk)
        acc = jnp.dot(og_ref[rows, :], wg_ref[...], preferred_element_type=F32)
        acc = acc + jnp.dot(or_ref[rows, :], wr_ref[...], preferred_element_type=F32)
        y = y_ref[rows, :]
        for g in range(SSD_G):
            lo, hi = g * SSD_GROUP_W, (g + 1) * SSD_GROUP_W
            yg = y[:, lo:hi]
            yg = yg * lax.rsqrt(jnp.mean(yg * yg, axis=-1, keepdims=True) + RMS_EPS) * nw[:, lo:hi]
            acc = acc + jnp.dot(yg.astype(BF16), wy_ref[lo:hi, :], preferred_element_type=F32)
        o_ref[rows, :] = _layer_norm_rows(DEEPNORM_ALPHA * x_ref[rows, :] + acc, g_ref[...], b_ref[...])


def _out_proj(x, o_gdn, o_ret, y_ssd, w_gdn, w_ret, w_ssd, ssd_norm_w, g, b):
    t, d = x.shape
    tm = PROJ_TOKEN_TILE
    rows = lambda width: pl.BlockSpec((tm, width), lambda i: (i, 0))
    whole = lambda a: pl.BlockSpec(a.shape, lambda i: (0, 0))
    return pl.pallas_call(
        _out_proj_kernel,
        grid=(t // tm,),
        in_specs=[
            rows(d), rows(GDN_W), rows(RET_W), rows(SSD_W),
            whole(w_gdn), whole(w_ret), whole(w_ssd), whole(ssd_norm_w), whole(g), whole(b),
        ],
        out_specs=rows(d),
        out_shape=jax.ShapeDtypeStruct((t, d), F32),
        compiler_params=_cparams(("parallel",)),
        name="out_proj",
    )(x, o_gdn, o_ret, y_ssd, w_gdn, w_ret, w_ssd, ssd_norm_w, g, b)


def _small_lane_sources():
    col = -np.ones((LANES,), np.int64)
    par = -np.ones((LANES,), np.int64)
    for h in range(GDN_HEADS):
        for d in range(2):
            col[4 * h + d] = _OFF_GDN_B + d * GDN_HEADS + h
            col[4 * h + 2 + d] = _OFF_GDN_A + d * GDN_HEADS + h
            par[4 * h + 2 + d] = d * GDN_HEADS + h
    for p in range(SSD_PAIRS):
        for d in range(2):
            for e in range(2):
                lane = SMALL_SSD_LANE0 + 4 * p + 2 * d + e
                col[lane] = _OFF_SSD_DT + d * SSD_HEADS + 2 * p + e
                par[lane] = 2 * GDN_HEADS + d * SSD_HEADS + 2 * p + e
    return col, par


def _permute_w_in(w):
    col, _ = _small_lane_sources()
    small = jnp.where(jnp.asarray(col >= 0)[None, :], w[:, np.maximum(col, 0)], 0.0)
    xbc = _OFF_SSD_XBC
    return jnp.concatenate([
        w[:, xbc:xbc + SSD_W],
        w[:, _OFF_SSD_Z:_OFF_SSD_Z + SSD_W],
        w[:, _OFF_GDN_QKV:_OFF_GDN_B],
        w[:, _OFF_RET:_OFF_SSD_XBC],
        w[:, xbc + SSD_W:_OFF_SSD_Z],
        small,
    ], axis=1).astype(BF16)


def _small_lane_vector(gdn_param, ssd_param):
    _, par = _small_lane_sources()
    flat = jnp.concatenate([gdn_param.reshape(-1), ssd_param.reshape(-1)]).astype(F32)
    return jnp.where(jnp.asarray(par >= 0), flat[np.maximum(par, 0)], 0.0).reshape(1, LANES)


def _rotary_tables(seq):
    inv = ROPE_BASE ** (-jnp.arange(0, RET_DK, 2, dtype=F32) / RET_DK)
    ang = jnp.arange(seq, dtype=F32)[:, None] * inv[None, :]
    cos, sin = jnp.cos(ang), jnp.sin(ang)
    return jnp.concatenate([cos, cos], axis=1), jnp.concatenate([-sin, sin], axis=1)


def _trunk(x3, layers):
    bsz, seq, d = x3.shape
    x = x3.reshape(bsz * seq, d)
    cos_t, sin_t = _rotary_tables(seq)
    log_gamma = jnp.log1p(-jnp.exp2(-5.0 - jnp.arange(RET_HEADS, dtype=F32)))
    for p in layers:
        x = _ffn(x, *p["ffn1"], *p["ln1"])
        proj = _in_proj(x, p["w_in"])
        o_gdn = _gdn(proj, p["gdn_conv_w"], p["small_bias"], p["small_alog"], p["gdn_norm_w"], bsz, seq)
        o_ret = _ret(proj, log_gamma, cos_t, sin_t, p["ret_norm_g"], p["ret_norm_b"], bsz, seq)
        y_ssd = _ssd(proj, p["ssd_conv_w"], p["ssd_conv_b"], p["small_bias"], p["small_alog"], p["ssd_d"], bsz, seq)
        x = _out_proj(x, o_gdn, o_ret, y_ssd, *p["w_out"], p["ssd_norm_w"], *p["ln2"])
        x = _ffn(x, *p["ffn2"], *p["ln3"])
    return x.reshape(bsz, seq, d)


def kernel(x_prompt, x_sample, w_in, w_out, gdn_conv_w, gdn_a_log, gdn_dt_bias, gdn_norm_w, ret_norm_g, ret_norm_b,
           ssd_conv_w, ssd_conv_b, ssd_a_log, ssd_dt_bias, ssd_d, ssd_norm_w,
           ffn1_w_gate, ffn1_w_up, ffn1_w_down, ffn2_w_gate, ffn2_w_up, ffn2_w_down,
           ln1_g, ln1_b, ln2_g, ln2_b, ln3_g, ln3_b):
    row = lambda a: a.astype(F32).reshape(1, -1)
    layers = []
    for l in range(DEPTH):
        wo = w_out[l].astype(BF16)
        layers.append(dict(
            ffn1=(ffn1_w_gate[l].astype(BF16), ffn1_w_up[l].astype(BF16), ffn1_w_down[l].astype(BF16)),
            ffn2=(ffn2_w_gate[l].astype(BF16), ffn2_w_up[l].astype(BF16), ffn2_w_down[l].astype(BF16)),
            ln1=(row(ln1_g[l]), row(ln1_b[l])),
            ln2=(row(ln2_g[l]), row(ln2_b[l])),
            ln3=(row(ln3_g[l]), row(ln3_b[l])),
            w_in=_permute_w_in(w_in[l]),
            w_out=(wo[:GDN_W], wo[GDN_W:GDN_W + RET_W], wo[GDN_W + RET_W:]),
            gdn_conv_w=gdn_conv_w[l].astype(F32),
            small_bias=_small_lane_vector(gdn_dt_bias[l], ssd_dt_bias[l]),
            small_alog=_small_lane_vector(gdn_a_log[l], ssd_a_log[l]),
            gdn_norm_w=row(gdn_norm_w[l]),
            ret_norm_g=row(ret_norm_g[l]),
            ret_norm_b=row(ret_norm_b[l]),
            ssd_conv_w=ssd_conv_w[l].astype(F32),
            ssd_conv_b=row(ssd_conv_b[l]),
            ssd_d=jnp.repeat(ssd_d[l].astype(F32), SSD_P).reshape(1, SSD_W),
            ssd_norm_w=row(ssd_norm_w[l]),
        ))
    return _trunk(x_prompt, layers), _trunk(x_sample, layers)
```

```python
import functools
import math

import jax
import jax.numpy as jnp
import numpy as np
from jax import lax
from jax.experimental import pallas as pl
from jax.experimental.pallas import tpu as pltpu

F32 = jnp.float32
BF16 = jnp.bfloat16

D_MODEL = 2048
DEPTH = 2
CONV_K = 5
GDN_HEADS = 6
GDN_DK = 128
RET_HEADS = 4
RET_DK = 128
SSD_HEADS = 12
SSD_P = 64
SSD_N = 128
SSD_G = 2
D_FF = 5632
ROPE_BASE = 10000.0
LN_EPS = 1e-5
RMS_EPS = 1e-6
DEEPNORM_ALPHA = (2 * DEPTH) ** 0.25

GDN_W = GDN_HEADS * 128
RET_W = RET_HEADS * 128
SSD_W = SSD_HEADS * SSD_P
SSD_GROUP_W = SSD_W // SSD_G
SSD_PAIRS = SSD_HEADS // 2

LANES = 128
SUBLANES = 8
VMEM_LIMIT_BYTES = 60000 * 1024

CHUNK = 128
CONV_HALO = 8
LOCKSTEP_CHUNKS = 4
DEEP_LOCKSTEP_CHUNKS = 8

CB_SSD_X = 0
CB_SSD_Z = 6
CB_GDN_Q = 12
CB_GDN_K = 18
CB_GDN_V = 24
CB_GDN_Z = 30
CB_RET_Q = 36
CB_RET_K = 40
CB_RET_V = 44
CB_RET_G = 48
CB_SSD_B = 52
CB_SSD_C = 54
CB_SMALL = 56
N_COL_BLOCKS = 57
IN_PROJ_COL_GROUPS = 3
COL_BLOCKS_PER_GROUP = N_COL_BLOCKS // IN_PROJ_COL_GROUPS
SMALL_GDN_LANES = 4 * GDN_HEADS
SMALL_SSD_LANE0 = 32
SMALL_SSD_LANES = 4 * SSD_PAIRS

_OFF_GDN_QKV = 0
_OFF_GDN_Z = 2304
_OFF_GDN_B = 3072
_OFF_GDN_A = 3084
_OFF_RET = 3096
_OFF_SSD_XBC = 5144
_OFF_SSD_Z = 6424
_OFF_SSD_DT = 7192

FFN_TOKEN_TILE = 1024
FFN_FF_TILE = 512
PROJ_TOKEN_TILE = 512
WEIGHT_CAST_BLOCK_BYTES = 6 * 1024 * 1024


def _cparams(semantics):
    return pltpu.CompilerParams(dimension_semantics=semantics, vmem_limit_bytes=VMEM_LIMIT_BYTES)


def _mm(a, b):
    return jnp.dot(a.astype(BF16), b.astype(BF16), preferred_element_type=F32)


def _mm_nt(a, b):
    return lax.dot_general(a.astype(BF16), b.astype(BF16), (((1,), (1,)), ((), ())), preferred_element_type=F32)


def _mm_tn(a, b):
    return lax.dot_general(a.astype(BF16), b.astype(BF16), (((0,), (0,)), ((), ())), preferred_element_type=F32)


def _silu(x):
    return x * jax.nn.sigmoid(x)


def _softplus(x):
    return jnp.maximum(x, 0.0) + jnp.log1p(jnp.exp(-jnp.abs(x)))


def _layer_norm_rows(y, g, b):
    mu = jnp.mean(y, axis=-1, keepdims=True)
    yc = y - mu
    var = jnp.mean(yc * yc, axis=-1, keepdims=True)
    return yc * lax.rsqrt(var + LN_EPS) * g + b


def _ffn_kernel(x_ref, wg_ref, wu_ref, wd_ref, g_ref, b_ref, o_ref):
    j = pl.program_id(1)

    @pl.when(j == 0)
    def _():
        o_ref[...] = jnp.zeros_like(o_ref)

    xb = x_ref[...].astype(BF16)
    gate = jnp.dot(xb, wg_ref[...], preferred_element_type=F32)
    up = jnp.dot(xb, wu_ref[...], preferred_element_type=F32)
    h = (_silu(gate) * up).astype(BF16)
    o_ref[...] += jnp.dot(h, wd_ref[...], preferred_element_type=F32)

    @pl.when(j == pl.num_programs(1) - 1)
    def _():
        y = DEEPNORM_ALPHA * x_ref[...] + 0.5 * o_ref[...]
        o_ref[...] = _layer_norm_rows(y, g_ref[...], b_ref[...])


def _ffn(x, wg, wu, wd, g, b):
    t, d = x.shape
    f = wg.shape[1]
    tm, tf = FFN_TOKEN_TILE, FFN_FF_TILE
    return pl.pallas_call(
        _ffn_kernel,
        grid=(t // tm, f // tf),
        in_specs=[
            pl.BlockSpec((tm, d), lambda i, j: (i, 0)),
            pl.BlockSpec((d, tf), lambda i, j: (0, j)),
            pl.BlockSpec((d, tf), lambda i, j: (0, j)),
            pl.BlockSpec((tf, d), lambda i, j: (j, 0)),
            pl.BlockSpec((1, d), lambda i, j: (0, 0)),
            pl.BlockSpec((1, d), lambda i, j: (0, 0)),
        ],
        out_specs=pl.BlockSpec((tm, d), lambda i, j: (i, 0)),
        out_shape=jax.ShapeDtypeStruct((t, d), F32),
        compiler_params=_cparams(("parallel", "arbitrary")),
        name="ffn",
    )(x, wg, wu, wd, g, b)


def _in_proj_kernel(x_ref, w_ref, o_ref):
    r = jnp.dot(x_ref[...].astype(BF16), w_ref[...], preferred_element_type=F32)
    for c in range(COL_BLOCKS_PER_GROUP):
        o_ref[c] = r[:, c * LANES:(c + 1) * LANES]


def _in_proj(x, w):
    t, d = x.shape
    tm = PROJ_TOKEN_TILE
    tn = COL_BLOCKS_PER_GROUP * LANES
    return pl.pallas_call(
        _in_proj_kernel,
        grid=(IN_PROJ_COL_GROUPS, t // tm),
        in_specs=[
            pl.BlockSpec((tm, d), lambda j, i: (i, 0)),
            pl.BlockSpec((d, tn), lambda j, i: (0, j)),
        ],
        out_specs=pl.BlockSpec((COL_BLOCKS_PER_GROUP, tm, LANES), lambda j, i: (j, i, 0)),
        out_shape=jax.ShapeDtypeStruct((N_COL_BLOCKS, t, LANES), F32),
        compiler_params=_cparams(("arbitrary", "arbitrary")),
        name="in_proj",
    )(x, w)


def _lockstep(gens):
    results = [None] * len(gens)
    live = list(enumerate(gens))
    while live:
        still = []
        for idx, gen in live:
            try:
                next(gen)
                still.append((idx, gen))
            except StopIteration as stop:
                results[idx] = stop.value
        live = still
    return results


def _chunk_loop(n, chunk_gen):
    width = math.gcd(n, LOCKSTEP_CHUNKS)

    def trip(t, carry):
        _lockstep([chunk_gen(t * width + u) for u in range(width)])
        return carry

    lax.fori_loop(0, n // width, trip, 0)


def _rows(ci):
    return pl.ds(pl.multiple_of(ci * CHUNK, CHUNK), CHUNK)


def _iota2(shape, axis):
    return lax.broadcasted_iota(jnp.int32, shape, axis)


def _tri_masks():
    i = _iota2((CHUNK, CHUNK), 0)
    j = _iota2((CHUNK, CHUNK), 1)
    return ((j <= i, j < i), (j >= i, j > i))


def _cumsum_mat():
    j = _iota2((CHUNK, 3 * CHUNK), 0)
    i = _iota2((CHUNK, 3 * CHUNK), 1)
    prefix = jnp.logical_and(i < CHUNK, j <= i)
    suffix_or_total = jnp.logical_and(i >= CHUNK, jnp.logical_or(j + CHUNK >= i, i >= 2 * CHUNK))
    return jnp.where(jnp.logical_or(prefix, suffix_or_total), 1.0, 0.0).astype(BF16)


def _chunk_sums(rows_list, cmat):
    hi = [r.astype(BF16).astype(F32) for r in rows_list]
    mid = [(r - h).astype(BF16).astype(F32) for r, h in zip(rows_list, hi)]
    lo = [r - h - m for r, h, m in zip(rows_list, hi, mid)]
    sums = jnp.dot(jnp.concatenate(hi + mid + lo, axis=0).astype(BF16), cmat, preferred_element_type=F32)
    k = len(rows_list)
    out = []
    for u in range(k):
        s = sum(sums[(part * k + u) * SUBLANES:(part * k + u + 1) * SUBLANES] for part in range(3))
        out.append((s[:, :CHUNK], s[:, CHUNK:2 * CHUNK], s[:, 2 * CHUNK:]))
    return out


def _lane_vector_rows(vec, tmp_ref, row0):
    tmp_ref[...] = jnp.broadcast_to(vec, (CHUNK, LANES)).T
    return tmp_ref[pl.ds(row0, SUBLANES), :]


def _gate_rows(sm_ref, tmp_ref, rrow_ref, rcol_ref, row0, row_values, assemble):
    cmat = _cumsum_mat()

    def trip(chunks):
        for u, ci in enumerate(chunks):
            tmp_ref[u] = sm_ref[_rows(ci), :].T
        yield
        rows = [row_values(tmp_ref[u, pl.ds(row0, SUBLANES), :]) for u in range(len(chunks))]
        yield
        sums = _chunk_sums(rows, cmat)
        yield
        r16s = [assemble(r, *s) for r, s in zip(rows, sums)]
        for ci, r16 in zip(chunks, r16s):
            rrow_ref[pl.ds(pl.multiple_of(ci * 16, 16), 16), :] = r16
        for ci, r16 in zip(chunks, r16s):
            rcol_ref[_rows(ci), :] = _pad_rows_transpose(r16)

    return trip


def _conv_silu(x_ref, xp_ref, w_ref, bias, dst_ref, seq, l2_scale=None, companion=None):
    n = seq // CHUNK
    zeros = jnp.zeros((CONV_HALO, LANES), F32)
    xp_ref[0:CONV_HALO, :] = zeros
    xp_ref[seq + CONV_HALO:seq + 2 * CONV_HALO, :] = zeros

    def copy(ci):
        r0 = pl.multiple_of(ci * CHUNK, CHUNK)
        xp_ref[pl.ds(r0 + CONV_HALO, CHUNK), :] = x_ref[pl.ds(r0, CHUNK), :]
        return
        yield

    _chunk_loop(n, copy)
    w = w_ref[...]

    def conv(ci):
        r0 = pl.multiple_of(ci * CHUNK, CHUNK)
        acc = jnp.zeros((CHUNK, LANES), F32)
        for k in range(CONV_K):
            acc = acc + xp_ref[pl.ds(r0 + (CONV_HALO - CONV_K // 2 + k), CHUNK), :] * w[k:k + 1, :]
        if bias is not None:
            acc = acc + bias
        t = _silu(acc)
        if l2_scale is not None:
            ss = jnp.sum(t * t, axis=-1, keepdims=True)
            yield
            t = t * (lax.rsqrt(ss + RMS_EPS) * l2_scale)
        dst_ref[pl.ds(r0, CHUNK), :] = t
        return
        yield

    if companion is None:
        _chunk_loop(n, conv)
        return

    width = math.gcd(n, DEEP_LOCKSTEP_CHUNKS)

    def conv_series(chunks):
        for ci in chunks:
            yield from conv(ci)
            yield

    def trip(t, carry):
        chunks = [t * width + u for u in range(width)]
        _lockstep([companion(chunks), conv_series(chunks[0::2]), conv_series(chunks[1::2])])
        return carry

    lax.fori_loop(0, n // width, trip, 0)


def _pad_rows_transpose(r16):
    padded = jnp.concatenate([r16, jnp.zeros((CHUNK - r16.shape[0], LANES), F32)], axis=0)
    return padded.T


def _unit_tri_inverse(lmat):
    i = _iota2((CHUNK, CHUNK), 0)
    j = _iota2((CHUNK, CHUNK), 1)

    def same(shift):
        return lax.shift_right_logical(i, shift) == lax.shift_right_logical(j, shift)

    eye = jnp.where(i == j, 1.0, 0.0).astype(F32)
    l16 = jnp.where(same(4), lmat, 0.0)
    x = eye - l16
    p = _mm(l16, l16)
    yield
    for _ in range(2):
        x, p = x + _mm(x, p), _mm(p, p)
        yield
    x = x + _mm(x, p)
    yield
    for shift in (5, 6, 7):
        off = jnp.where(jnp.logical_and(same(shift), jnp.logical_not(same(shift - 1))), lmat, 0.0)
        ex = _mm(off, x)
        yield
        x = x - _mm(x, ex)
        yield
    return x


def _gdn_chunk_terms(gram_kk, gram_qk, q, k, v, rt, r16, d, masks):
    incl, strict = masks[d]
    beta = rt[:, d:d + 1]
    gc_col = rt[:, 2 + d:3 + d]
    gc_row = r16[2 + d:3 + d, :]
    tot_col = rt[:, 10 + d:11 + d]
    decay = jnp.where(incl, jnp.exp(jnp.where(incl, gc_col - gc_row, 0.0)), 0.0)
    lmat = jnp.where(strict, gram_kk * beta * decay, 0.0)
    tinv = yield from _unit_tri_inverse(lmat)
    eg = jnp.exp(gc_col)
    sol = _mm(tinv, jnp.concatenate([v * beta, k * (beta * eg)], axis=1)).astype(BF16)
    yield
    kt = _mm_tn(k * jnp.exp(tot_col - gc_col), sol)
    qs = _mm(gram_qk * decay, sol)
    yield
    return kt[:, LANES:], kt[:, :LANES], q * eg - qs[:, LANES:], qs[:, :LANES]


def _gdn_kernel(q_ref, k_ref, v_ref, z_ref, sm_ref, cwq_ref, cwk_ref, cwv_ref, bias_ref, alog_ref, nw_ref,
                o_ref, xp_ref, qn_ref, kn_ref, vn_ref, rrow_ref, rcol_ref, of_ref, ob_ref, mq_ref, nn_ref,
                tmp_ref, *, seq):
    n = seq // CHUNK
    h = pl.program_id(1)

    row0 = 4 * h
    row = _iota2((SUBLANES, LANES), 0)
    bias8 = _lane_vector_rows(bias_ref[...], tmp_ref.at[0], row0)
    neg_a8 = -jnp.exp(_lane_vector_rows(alog_ref[...], tmp_ref.at[0], row0))

    def gate_values(raw8):
        return jnp.where(row < 2, jax.nn.sigmoid(raw8), neg_a8 * _softplus(raw8 + bias8))

    def gate_rows(vals, prefix, suffix, total):
        return jnp.concatenate([jnp.where(row < 2, vals, jnp.where(row == 2, prefix, suffix)), total], axis=0)

    gates = _gate_rows(sm_ref, tmp_ref, rrow_ref, rcol_ref, row0, gate_values, gate_rows)
    _conv_silu(q_ref, xp_ref, cwq_ref, None, qn_ref, seq, l2_scale=GDN_DK ** -0.5, companion=gates)
    _conv_silu(k_ref, xp_ref, cwk_ref, None, kn_ref, seq, l2_scale=1.0)
    _conv_silu(v_ref, xp_ref, cwv_ref, None, vn_ref, seq)

    masks = _tri_masks()
    outs = (of_ref, ob_ref)

    def local_terms(d, ci):
        q = qn_ref[_rows(ci), :]
        k = kn_ref[_rows(ci), :]
        v = vn_ref[_rows(ci), :]
        rt = rcol_ref[_rows(ci), :]
        r16 = rrow_ref[pl.ds(pl.multiple_of(ci * 16, 16), 16), :]
        grams = _mm_nt(jnp.concatenate([k, q], axis=0), k)
        yield
        m_, n_, q_, o_ = yield from _gdn_chunk_terms(grams[:CHUNK], grams[CHUNK:], q, k, v, rt, r16, d, masks)
        mq_ref[d, pl.ds(pl.multiple_of(2 * ci * CHUNK, CHUNK), 2 * CHUNK), :] = (
            jnp.concatenate([m_, q_], axis=0).astype(BF16))
        nn_ref[d, _rows(ci), :] = n_
        outs[d][_rows(ci), :] = o_

    def advance(d, chunks, state):
        for ci in chunks:
            m = jnp.dot(mq_ref[d, pl.ds(pl.multiple_of(2 * ci * CHUNK, CHUNK), 2 * CHUNK), :], state.astype(BF16),
                        preferred_element_type=F32)
            yield
            outs[d][_rows(ci), :] += m[CHUNK:]
            decay = jnp.exp(jnp.broadcast_to(rrow_ref[pl.ds(ci * 16 + 10 + d, 1), :], (GDN_DK, LANES)))
            state = state * decay - m[:CHUNK] + nn_ref[d, _rows(ci), :]
        return state

    width = math.gcd(n, DEEP_LOCKSTEP_CHUNKS)
    trips = n // width
    sweep_chunks = lambda t: ([t * width + u for u in range(width)], [n - 1 - (t * width + u) for u in range(width)])

    def terms_of(t):
        fwd, bwd = sweep_chunks(t)
        return [local_terms(0, ci) for ci in fwd] + [local_terms(1, ci) for ci in bwd]

    def sweeps_of(t, states):
        fwd, bwd = sweep_chunks(t)
        return [advance(0, fwd, states[0]), advance(1, bwd, states[1])]

    def first_trip(t, states):
        _lockstep(terms_of(t))
        return states

    def middle_trip(t, states):
        return tuple(_lockstep(sweeps_of(t - 1, states) + terms_of(t))[:2])

    def last_sweep(t, states):
        return tuple(_lockstep(sweeps_of(t, states)))

    zero_state = jnp.zeros((GDN_DK, LANES), F32)
    states = lax.fori_loop(0, 1, first_trip, (zero_state, zero_state))
    states = lax.fori_loop(1, trips, middle_trip, states)
    lax.fori_loop(trips - 1, trips, last_sweep, states)

    nw = nw_ref[...]

    def finish(ci):
        o = of_ref[_rows(ci), :] + ob_ref[_rows(ci), :]
        ms = jnp.mean(o * o, axis=-1, keepdims=True)
        yield
        o = o * (lax.rsqrt(ms + RMS_EPS) * nw)
        o_ref[_rows(ci), :] = (o * _silu(z_ref[_rows(ci), :])).astype(o_ref.dtype)

    _chunk_loop(n, finish)


def _proj_block(cb0):
    return lambda seq: pl.BlockSpec((None, seq, LANES), lambda b, h: (cb0 + h, b, 0))


def _gdn(proj, conv_w, bias_vec, alog_vec, norm_w, bsz, seq):
    n = seq // CHUNK
    col = lambda c0: pl.BlockSpec((CONV_K, LANES), lambda b, h: (0, c0 + h))
    vec = pl.BlockSpec((1, LANES), lambda b, h: (0, 0))
    seq_f32 = pltpu.VMEM((seq, LANES), F32)
    return pl.pallas_call(
        functools.partial(_gdn_kernel, seq=seq),
        grid=(bsz, GDN_HEADS),
        in_specs=[
            _proj_block(CB_GDN_Q)(seq), _proj_block(CB_GDN_K)(seq), _proj_block(CB_GDN_V)(seq),
            _proj_block(CB_GDN_Z)(seq),
            pl.BlockSpec((None, seq, LANES), lambda b, h: (CB_SMALL, b, 0)),
            col(0), col(GDN_HEADS), col(2 * GDN_HEADS), vec, vec, vec,
        ],
        out_specs=pl.BlockSpec((seq, LANES), lambda b, h: (b, h)),
        out_shape=jax.ShapeDtypeStruct((bsz * seq, GDN_W), BF16),
        scratch_shapes=[
            pltpu.VMEM((seq + 2 * CONV_HALO, LANES), F32),
            seq_f32, seq_f32, seq_f32,
            pltpu.VMEM((n * 16, LANES), F32),
            seq_f32, seq_f32, seq_f32,
            pltpu.VMEM((2, 2 * seq, LANES), BF16),
            pltpu.VMEM((2, seq, LANES), F32),
            pltpu.VMEM((DEEP_LOCKSTEP_CHUNKS, CHUNK, LANES), F32),
        ],
        compiler_params=_cparams(("parallel", "arbitrary")),
        name="gdn",
    )(proj, proj, proj, proj, proj, conv_w, conv_w, conv_w, bias_vec, alog_vec, norm_w)


def _ret_kernel(lg_ref, q_ref, k_ref, v_ref, g_ref, cos_ref, sin_ref, ng_ref, nb_ref,
                o_ref, of_ref, ob_ref, *, seq):
    n = seq // CHUNK
    width = math.gcd(n, LOCKSTEP_CHUNKS)
    lg = lg_ref[pl.program_id(1)]
    i = _iota2((CHUNK, CHUNK), 0).astype(F32)
    j = _iota2((CHUNK, CHUNK), 1).astype(F32)
    d_sym = jnp.exp(jnp.abs(i - j) * lg)
    q_dec = (jnp.exp((i + 1.0) * lg), jnp.exp((CHUNK - i) * lg))
    k_dec = (jnp.exp((CHUNK - 1.0 - i) * lg), jnp.exp(i * lg))
    chunk_decay = jnp.exp(jnp.full((RET_DK, LANES), CHUNK, F32) * lg)
    outs = (of_ref, ob_ref)

    def rot(t, ci):
        return t * cos_ref[_rows(ci), :] + pltpu.roll(t, LANES // 2, axis=1) * sin_ref[_rows(ci), :]

    def chunk_terms(d, ci):
        q = rot(q_ref[_rows(ci), :], ci)
        k = rot(k_ref[_rows(ci), :], ci) * (RET_DK ** -0.5)
        v = v_ref[_rows(ci), :]
        kv = _mm_tn(k * k_dec[d], v)
        scores = _mm_nt(q, k) * d_sym if d == 0 else None
        yield
        intra = _mm(scores, v) if d == 0 else None
        return q, kv, intra

    def sweep(t, carry):
        chunks = ([t * width + u for u in range(width)], [n - 1 - (t * width + u) for u in range(width)])
        terms = _lockstep([chunk_terms(d, ci) for d in range(2) for ci in chunks[d]])
        states = list(carry)
        for d in range(2):
            for u, ci in enumerate(chunks[d]):
                q, kv, intra = terms[d * width + u]
                inter = _mm(q * q_dec[d], states[d])
                outs[d][_rows(ci), :] = inter if intra is None else intra + inter
                states[d] = states[d] * chunk_decay + kv
        return tuple(states)

    zero_state = jnp.zeros((RET_DK, LANES), F32)
    lax.fori_loop(0, n // width, sweep, (zero_state, zero_state))

    ng = ng_ref[...]
    nb = nb_ref[...]

    def finish(ci):
        o = of_ref[_rows(ci), :] + ob_ref[_rows(ci), :]
        mu = jnp.mean(o, axis=-1, keepdims=True)
        yield
        oc = o - mu
        var = jnp.mean(oc * oc, axis=-1, keepdims=True)
        yield
        o = oc * (lax.rsqrt(var + LN_EPS) * ng) + nb
        o_ref[_rows(ci), :] = (_silu(g_ref[_rows(ci), :]) * o).astype(o_ref.dtype)

    _chunk_loop(n, finish)


def _ret(proj, log_gamma, cos_t, sin_t, norm_g, norm_b, bsz, seq):
    table = pl.BlockSpec((seq, LANES), lambda b, h: (0, 0))
    vec = pl.BlockSpec((1, LANES), lambda b, h: (0, h))
    seq_f32 = pltpu.VMEM((seq, LANES), F32)
    return pl.pallas_call(
        functools.partial(_ret_kernel, seq=seq),
        grid=(bsz, RET_HEADS),
        in_specs=[
            pl.BlockSpec(memory_space=pltpu.SMEM),
            _proj_block(CB_RET_Q)(seq), _proj_block(CB_RET_K)(seq), _proj_block(CB_RET_V)(seq),
            _proj_block(CB_RET_G)(seq),
            table, table, vec, vec,
        ],
        out_specs=pl.BlockSpec((seq, LANES), lambda b, h: (b, h)),
        out_shape=jax.ShapeDtypeStruct((bsz * seq, RET_W), BF16),
        scratch_shapes=[seq_f32, seq_f32],
        compiler_params=_cparams(("parallel", "arbitrary")),
        name="ret",
    )(log_gamma, proj, proj, proj, proj, cos_t, sin_t, norm_g, norm_b)


def _ssd_kernel(x_ref, z_ref, b_ref, c_ref, sm_ref, cwx_ref, cwb_ref, cwc_ref, cbx_ref, cbb_ref, cbc_ref,
                bias_ref, alog_ref, dvec_ref, o_ref, xp_ref, xs_ref, bm_ref, cm_ref, rrow_ref, rcol_ref,
                of_ref, ob_ref, st_ref, sc_ref, tmp_ref, *, seq):
    n = seq // CHUNK
    width = math.gcd(n, LOCKSTEP_CHUNKS)
    p = pl.program_id(1)

    row0 = SMALL_SSD_LANE0 + 4 * p
    row = _iota2((SUBLANES, LANES), 0)

    def rows_twice(x8):
        return jnp.where(row < 4, x8, pltpu.roll(x8, 4, axis=0))

    bias8 = _lane_vector_rows(bias_ref[...], tmp_ref.at[0], row0)
    neg_a8 = rows_twice(-jnp.exp(_lane_vector_rows(alog_ref[...], tmp_ref.at[0], row0)))

    def gate_values(raw8):
        dt = rows_twice(_softplus(raw8 + bias8))
        return jnp.where(row < 4, dt, dt * neg_a8)

    def gate_rows(vals, prefix, suffix, total):
        return jnp.concatenate([jnp.where(row < 4, vals, jnp.where(row < 6, prefix, suffix)), total], axis=0)

    gates = _gate_rows(sm_ref, tmp_ref, rrow_ref, rcol_ref, row0, gate_values, gate_rows)
    _conv_silu(x_ref, xp_ref, cwx_ref, cbx_ref[...], xs_ref, seq, companion=gates)
    _conv_silu(b_ref, xp_ref, cwb_ref, cbb_ref[...], bm_ref, seq)
    _conv_silu(c_ref, xp_ref, cwc_ref, cbc_ref[...], cm_ref, seq)

    masks = _tri_masks()
    first = _iota2((CHUNK, LANES), 1) < SSD_P
    outs = (of_ref, ob_ref)

    def total_rows(ci, d):
        return [jnp.broadcast_to(rrow_ref[pl.ds(ci * 16 + 12 + 2 * d + e, 1), :], (CHUNK, LANES)) for e in range(2)]

    def local_terms(ci):
        x = xs_ref[_rows(ci), :]
        bm = bm_ref[_rows(ci), :]
        cm = cm_ref[_rows(ci), :]
        rt = rcol_ref[_rows(ci), :]
        r16 = rrow_ref[pl.ds(pl.multiple_of(ci * 16, 16), 16), :]
        cb = _mm_nt(cm, bm)
        bm_t = bm.T
        x_heads = jnp.concatenate([jnp.where(first, x, 0.0), jnp.where(first, 0.0, x)], axis=0).astype(BF16)
        yield
        for d in range(2):
            incl = masks[d][0]
            tot = total_rows(ci, d)
            lc_cols, m_parts, g_parts = [], [], []
            for e in range(2):
                dt_row = r16[2 * d + e:2 * d + e + 1, :]
                lc_row = r16[4 + 2 * d + e:5 + 2 * d + e, :]
                lc_col = jnp.broadcast_to(rt[:, 4 + 2 * d + e:5 + 2 * d + e], (CHUNK, LANES))
                seg = jnp.where(incl, jnp.exp(jnp.where(incl, lc_col - lc_row, 0.0)), 0.0)
                m_parts.append(cb * seg * dt_row)
                g_parts.append(bm_t * (dt_row * jnp.exp(tot[e][0:1, :] - lc_row)))
                lc_cols.append(lc_col)
            lhs = jnp.concatenate([jnp.concatenate(m_parts, axis=1), jnp.concatenate(g_parts, axis=1)], axis=0)
            both = jnp.dot(lhs.astype(BF16), x_heads, preferred_element_type=F32)
            outs[d][_rows(ci), :] = both[:CHUNK]
            st_ref[d, _rows(ci), :] = both[CHUNK:]
            sc_ref[d, _rows(ci), :] = jnp.exp(jnp.where(first, lc_cols[0], lc_cols[1]))

    _chunk_loop(n, local_terms)

    def sweep(t, carry):
        chunks = ([t * width + u for u in range(width)], [n - 1 - (t * width + u) for u in range(width)])
        states = list(carry)
        for d in range(2):
            for ci in chunks[d]:
                outs[d][_rows(ci), :] += _mm(cm_ref[_rows(ci), :], states[d]) * sc_ref[d, _rows(ci), :]
                tot = total_rows(ci, d)
                states[d] = states[d] * jnp.exp(jnp.where(first, tot[0], tot[1])) + st_ref[d, _rows(ci), :]
        return tuple(states)

    zero_state = jnp.zeros((SSD_N, 2 * SSD_P), F32)
    lax.fori_loop(0, n // width, sweep, (zero_state, zero_state))

    dvec = dvec_ref[...]

    def finish(ci):
        y = of_ref[_rows(ci), :] + ob_ref[_rows(ci), :] + dvec * xs_ref[_rows(ci), :]
        o_ref[_rows(ci), :] = y * _silu(z_ref[_rows(ci), :])
        return
        yield

    _chunk_loop(n, finish)


def _ssd(proj, conv_w, conv_b, bias_vec, alog_vec, dvec, bsz, seq):
    n = seq // CHUNK
    group = lambda p: p // (SSD_PAIRS // SSD_G)
    wcol = lambda fn: pl.BlockSpec((CONV_K, LANES), lambda b, p: (0, fn(p)))
    bcol = lambda fn: pl.BlockSpec((1, LANES), lambda b, p: (0, fn(p)))
    vec = pl.BlockSpec((1, LANES), lambda b, p: (0, 0))
    seq_f32 = pltpu.VMEM((seq, LANES), F32)
    x_col = lambda p: p
    b_col = lambda p: SSD_W // LANES + group(p)
    c_col = lambda p: SSD_W // LANES + SSD_G + group(p)
    return pl.pallas_call(
        functools.partial(_ssd_kernel, seq=seq),
        grid=(bsz, SSD_PAIRS),
        in_specs=[
            _proj_block(CB_SSD_X)(seq), _proj_block(CB_SSD_Z)(seq),
            pl.BlockSpec((None, seq, LANES), lambda b, p: (CB_SSD_B + group(p), b, 0)),
            pl.BlockSpec((None, seq, LANES), lambda b, p: (CB_SSD_C + group(p), b, 0)),
            pl.BlockSpec((None, seq, LANES), lambda b, p: (CB_SMALL, b, 0)),
            wcol(x_col), wcol(b_col), wcol(c_col), bcol(x_col), bcol(b_col), bcol(c_col),
            vec, vec, bcol(x_col),
        ],
        out_specs=pl.BlockSpec((seq, LANES), lambda b, p: (b, p)),
        out_shape=jax.ShapeDtypeStruct((bsz * seq, SSD_W), F32),
        scratch_shapes=[
            pltpu.VMEM((seq + 2 * CONV_HALO, LANES), F32),
            seq_f32, seq_f32, seq_f32,
            pltpu.VMEM((n * 16, LANES), F32),
            seq_f32, seq_f32, seq_f32,
            pltpu.VMEM((2, seq, LANES), F32),
            pltpu.VMEM((2, seq, LANES), F32),
            pltpu.VMEM((DEEP_LOCKSTEP_CHUNKS, CHUNK, LANES), F32),
        ],
        compiler_params=_cparams(("parallel", "arbitrary")),
        name="ssd",
    )(proj, proj, proj, proj, proj, conv_w, conv_w, conv_w, conv_b, conv_b, conv_b, bias_vec, alog_vec, dvec)


def _out_proj_kernel(x_ref, og_ref, or_ref, y_ref, wg_ref, wr_ref, wy_ref, nw_ref, g_ref, b_ref, o_ref):
    acc = jnp.dot(og_ref[...], wg_ref[...], preferred_element_type=F32)
    acc = acc + jnp.dot(or_ref[...], wr_ref[...], preferred_element_type=F32)
    y = y_ref[...]
    nw = nw_ref[...]
    for g in range(SSD_G):
        lo, hi = g * SSD_GROUP_W, (g + 1) * SSD_GROUP_W
        yg = y[:, lo:hi]
        yg = yg * lax.rsqrt(jnp.mean(yg * yg, axis=-1, keepdims=True) + RMS_EPS) * nw[:, lo:hi]
        acc = acc + jnp.dot(yg.astype(BF16), wy_ref[lo:hi, :], preferred_element_type=F32)
    o_ref[...] = _layer_norm_rows(DEEPNORM_ALPHA * x_ref[...] + acc, g_ref[...], b_ref[...])


def _out_proj(x, o_gdn, o_ret, y_ssd, w_gdn, w_ret, w_ssd, ssd_norm_w, g, b):
    t, d = x.shape
    tm = PROJ_TOKEN_TILE
    rows = lambda width: pl.BlockSpec((tm, width), lambda i: (i, 0))
    whole = lambda a: pl.BlockSpec(a.shape, lambda i: (0, 0))
    return pl.pallas_call(
        _out_proj_kernel,
        grid=(t // tm,),
        in_specs=[
            rows(d), rows(GDN_W), rows(RET_W), rows(SSD_W),
            whole(w_gdn), whole(w_ret), whole(w_ssd), whole(ssd_norm_w), whole(g), whole(b),
        ],
        out_specs=rows(d),
        out_shape=jax.ShapeDtypeStruct((t, d), F32),
        compiler_params=_cparams(("parallel",)),
        name="out_proj",
    )(x, o_gdn, o_ret, y_ssd, w_gdn, w_ret, w_ssd, ssd_norm_w, g, b)


def _cast_kernel(w_ref, o_ref):
    o_ref[...] = w_ref[...].astype(o_ref.dtype)


def _layer_weight_bf16(w, layer):
    _, rows, cols = w.shape
    block_rows = 16
    while rows % (2 * block_rows) == 0 and 2 * block_rows * cols * 4 <= WEIGHT_CAST_BLOCK_BYTES:
        block_rows *= 2
    return pl.pallas_call(
        _cast_kernel,
        grid=(rows // block_rows,),
        in_specs=[pl.BlockSpec((None, block_rows, cols), lambda i: (layer, i, 0))],
        out_specs=pl.BlockSpec((block_rows, cols), lambda i: (i, 0)),
        out_shape=jax.ShapeDtypeStruct((rows, cols), BF16),
        compiler_params=_cparams(("arbitrary",)),
        name="weight_cast",
    )(w)


def _small_lane_sources():
    col = -np.ones((LANES,), np.int64)
    par = -np.ones((LANES,), np.int64)
    for h in range(GDN_HEADS):
        for d in range(2):
            col[4 * h + d] = _OFF_GDN_B + d * GDN_HEADS + h
            col[4 * h + 2 + d] = _OFF_GDN_A + d * GDN_HEADS + h
            par[4 * h + 2 + d] = d * GDN_HEADS + h
    for p in range(SSD_PAIRS):
        for d in range(2):
            for e in range(2):
                lane = SMALL_SSD_LANE0 + 4 * p + 2 * d + e
                col[lane] = _OFF_SSD_DT + d * SSD_HEADS + 2 * p + e
                par[lane] = 2 * GDN_HEADS + d * SSD_HEADS + 2 * p + e
    return col, par


def _permute_w_in(w):
    col, _ = _small_lane_sources()
    small = jnp.where(jnp.asarray(col >= 0)[None, :], w[:, np.maximum(col, 0)], 0.0)
    xbc = _OFF_SSD_XBC
    return jnp.concatenate([
        w[:, xbc:xbc + SSD_W],
        w[:, _OFF_SSD_Z:_OFF_SSD_Z + SSD_W],
        w[:, _OFF_GDN_QKV:_OFF_GDN_B],
        w[:, _OFF_RET:_OFF_SSD_XBC],
        w[:, xbc + SSD_W:_OFF_SSD_Z],
        small,
    ], axis=1).astype(BF16)


def _small_lane_vector(gdn_param, ssd_param):
    _, par = _small_lane_sources()
    flat = jnp.concatenate([gdn_param.reshape(-1), ssd_param.reshape(-1)]).astype(F32)
    return jnp.where(jnp.asarray(par >= 0), flat[np.maximum(par, 0)], 0.0).reshape(1, LANES)


def _rotary_tables(seq):
    inv = ROPE_BASE ** (-jnp.arange(0, RET_DK, 2, dtype=F32) / RET_DK)
    ang = jnp.arange(seq, dtype=F32)[:, None] * inv[None, :]
    cos, sin = jnp.cos(ang), jnp.sin(ang)
    return jnp.concatenate([cos, cos], axis=1), jnp.concatenate([-sin, sin], axis=1)


def _trunk(x3, layers):
    bsz, seq, d = x3.shape
    x = x3.reshape(bsz * seq, d)
    cos_t, sin_t = _rotary_tables(seq)
    log_gamma = jnp.log1p(-jnp.exp2(-5.0 - jnp.arange(RET_HEADS, dtype=F32)))
    for p in layers:
        x = _ffn(x, *p["ffn1"], *p["ln1"])
        proj = _in_proj(x, p["w_in"])
        o_gdn = _gdn(proj, p["gdn_conv_w"], p["small_bias"], p["small_alog"], p["gdn_norm_w"], bsz, seq)
        o_ret = _ret(proj, log_gamma, cos_t, sin_t, p["ret_norm_g"], p["ret_norm_b"], bsz, seq)
        y_ssd = _ssd(proj, p["ssd_conv_w"], p["ssd_conv_b"], p["small_bias"], p["small_alog"], p["ssd_d"], bsz, seq)
        x = _out_proj(x, o_gdn, o_ret, y_ssd, *p["w_out"], p["ssd_norm_w"], *p["ln2"])
        x = _ffn(x, *p["ffn2"], *p["ln3"])
    return x.reshape(bsz, seq, d)


def kernel(x_prompt, x_sample, w_in, w_out, gdn_conv_w, gdn_a_log, gdn_dt_bias, gdn_norm_w, ret_norm_g, ret_norm_b,
           ssd_conv_w, ssd_conv_b, ssd_a_log, ssd_dt_bias, ssd_d, ssd_norm_w,
           ffn1_w_gate, ffn1_w_up, ffn1_w_down, ffn2_w_gate, ffn2_w_up, ffn2_w_down,
           ln1_g, ln1_b, ln2_g, ln2_b, ln3_g, ln3_b):
    row = lambda a: a.astype(F32).reshape(1, -1)
    layers = []
    for l in range(DEPTH):
        wo = w_out[l].astype(BF16)
        layers.append(dict(
            ffn1=tuple(_layer_weight_bf16(w, l) for w in (ffn1_w_gate, ffn1_w_up, ffn1_w_down)),
            ffn2=tuple(_layer_weight_bf16(w, l) for w in (ffn2_w_gate, ffn2_w_up, ffn2_w_down)),
            ln1=(row(ln1_g[l]), row(ln1_b[l])),
            ln2=(row(ln2_g[l]), row(ln2_b[l])),
            ln3=(row(ln3_g[l]), row(ln3_b[l])),
            w_in=_permute_w_in(w_in[l]),
            w_out=(wo[:GDN_W], wo[GDN_W:GDN_W + RET_W], wo[GDN_W + RET_W:]),
            gdn_conv_w=gdn_conv_w[l].astype(F32),
            small_bias=_small_lane_vector(gdn_dt_bias[l], ssd_dt_bias[l]),
            small_alog=_small_lane_vector(gdn_a_log[l], ssd_a_log[l]),
            gdn_norm_w=row(gdn_norm_w[l]),
            ret_norm_g=row(ret_norm_g[l]),
            ret_norm_b=row(ret_norm_b[l]),
            ssd_conv_w=ssd_conv_w[l].astype(F32),
            ssd_conv_b=row(ssd_conv_b[l]),
            ssd_d=jnp.repeat(ssd_d[l].astype(F32), SSD_P).reshape(1, SSD_W),
            ssd_norm_w=row(ssd_norm_w[l]),
        ))
    return _trunk(x_prompt, layers), _trunk(x_sample, layers)
```
